```python
import math
import jax, jax.numpy as jnp
from jax import lax
import numpy as np

D_MODEL = 1024
BATCH = 2
SEQ = 8192
DEPTH = 4
DEC_BATCH = 128
DEC_SEQ = 4
PAST_LEN = 2048
PAGE_SIZE = 128

HEAD_DIM = 64
N_HEADS = D_MODEL // HEAD_DIM
N_DIFF_HEADS = D_MODEL // (4 * HEAD_DIM)
DIFF_SLOTS = 2 * N_DIFF_HEADS
N_SB_HEADS = D_MODEL // (2 * HEAD_DIM)
ROT_DIM = HEAD_DIM // 4
ROPE_THETA = 500000.0
DILATED_PATTERNS = ((128, 1), (512, 4), (2048, 16))
MAX_WINDOW = 2048
N_MEM = 256
MEM_HEADS = 4
MEM_HEAD_DIM = D_MODEL // MEM_HEADS
D_FF = -(-8 * D_MODEL // (3 * 256)) * 256
Q_BLOCK = 128
RMS_EPS = 1e-6
NEG_INF = -1e30
N_EVEN = (DEPTH + 1) // 2
N_ODD = DEPTH // 2

kernel_name = 'hybrid_diff_stick_dilated_step'


def _rms(x, g):
    xf = x.astype(jnp.float32)
    y = xf * lax.rsqrt(jnp.mean(xf * xf, axis=-1, keepdims=True) + RMS_EPS)
    return (y * g.astype(jnp.float32)).astype(x.dtype)


def _rope(x, pos):
    half = ROT_DIM // 2
    inv = jnp.power(jnp.float32(ROPE_THETA), -jnp.arange(half, dtype=jnp.float32) / half)
    ang = pos.astype(jnp.float32)[:, None] * inv[None, :]
    cos = jnp.cos(ang)[None, :, None, :]
    sin = jnp.sin(ang)[None, :, None, :]
    xr = x[..., :ROT_DIM].astype(jnp.float32)
    x1, x2 = xr[..., :half], xr[..., half:]
    rot = jnp.concatenate([x1 * cos - x2 * sin, x2 * cos + x1 * sin], axis=-1).astype(x.dtype)
    return jnp.concatenate([rot, x[..., ROT_DIM:]], axis=-1)


def _heads(z, n):
    return z.reshape(z.shape[0], z.shape[1], n, HEAD_DIM)


def _even_qkv(h, w, pos):
    qa, ka, va, qb, kb, vb = jnp.split(h @ w, 6, axis=-1)
    q = jnp.concatenate([_rope(_heads(qa, DIFF_SLOTS), pos), _heads(qb, N_SB_HEADS)], axis=2)
    k = jnp.concatenate([_rope(_heads(ka, DIFF_SLOTS), pos), _heads(kb, N_SB_HEADS)], axis=2)
    v = jnp.concatenate([_heads(va, DIFF_SLOTS), _heads(vb, N_SB_HEADS)], axis=2)
    return q, k, v


def _odd_qkv(h, w, pos):
    q, k, v = jnp.split(h @ w, 3, axis=-1)
    return _rope(_heads(q, N_HEADS), pos), _rope(_heads(k, N_HEADS), pos), _heads(v, N_HEADS)


def _sweep_blocks(fn, q, q_pos):
    b, t = q.shape[0], q.shape[1]
    if t > Q_BLOCK and t % Q_BLOCK == 0:
        nb = t // Q_BLOCK
        qb = q.reshape((b, nb, Q_BLOCK) + q.shape[2:]).swapaxes(0, 1)
        pb = q_pos.reshape(nb, Q_BLOCK)
        out = lax.map(lambda a: fn(a[0], a[1]), (qb, pb))
        return jax.tree_util.tree_map(lambda o: o.swapaxes(0, 1).reshape((b, t) + o.shape[3:]), out)
    return fn(q, q_pos)


def _ab_attend(q, q_pos, segs, k_pos, lam):
    b, tq = q.shape[0], q.shape[1]
    s = jnp.concatenate([jnp.einsum('bqhd,bkhd->bhqk', q, kk) for kk, _ in segs], axis=-1)
    s = s.astype(jnp.float32) * (HEAD_DIM ** -0.5)
    causal = k_pos[None, :] <= q_pos[:, None]
    strict = k_pos[None, :] < q_pos[:, None]
    sa = s[:, :DIFF_SLOTS].reshape(b, N_DIFF_HEADS, 2, tq, -1)
    pa = jax.nn.softmax(jnp.where(causal, sa, NEG_INF), axis=-1)
    wa = pa[:, :, 0] - lam * pa[:, :, 1]
    z = s[:, DIFF_SLOTS:]
    l_stay = jnp.where(strict, jax.nn.log_sigmoid(-z), 0.0)
    log_after = lax.cumsum(l_stay, axis=3, reverse=True) - l_stay
    wb = jnp.where(strict, jnp.exp(jax.nn.log_sigmoid(z) + log_after), 0.0)
    oa, ob, off = 0.0, 0.0, 0
    for _, vv in segs:
        n = vv.shape[1]
        va = vv[:, :, :DIFF_SLOTS].reshape(b, n, N_DIFF_HEADS, 2 * HEAD_DIM)
        vb = vv[:, :, DIFF_SLOTS:]
        oa = oa + jnp.einsum('bhqk,bkhe->bqhe', wa[..., off:off + n].astype(vv.dtype), va)
        ob = ob + jnp.einsum('bhqk,bkhe->bqhe', wb[..., off:off + n].astype(vv.dtype), vb)
        off += n
    return oa, ob


def _ab_merge(oa, ob, g, lam_init, w):
    b, t = oa.shape[0], oa.shape[1]
    oa = _rms(oa, g) * (1.0 - lam_init)
    return jnp.concatenate([oa.reshape(b, t, -1), ob.reshape(b, t, -1)], axis=-1) @ w


def _gather_rows(segs, pos):
    out = None
    for arr, base in segs:
        L = arr.shape[1]
        rel = pos - base
        g = jnp.take(arr, jnp.clip(rel, 0, L - 1), axis=1)
        if out is None:
            out = g
        else:
            inside = (rel >= 0) & (rel < L)
            out = jnp.where(inside[None, :, :, None, None], g, out)
    return out


def _dilated_attend(q, q_pos, k_segs, v_segs):
    lses, outs = [], []
    for window, dil in DILATED_PATTERNS:
        n = window // dil + 1
        kpos = q_pos[:, None] - dil * jnp.arange(n, dtype=jnp.int32)[None, :]
        valid = kpos >= 0
        kg = _gather_rows(k_segs, kpos)
        vg = _gather_rows(v_segs, kpos)
        s = jnp.einsum('bqhd,bqjhd->bhqj', q, kg).astype(jnp.float32) * (HEAD_DIM ** -0.5)
        s = jnp.where(valid[None, None], s, NEG_INF)
        m = jnp.max(s, axis=-1, keepdims=True)
        e = jnp.exp(s - m)
        den = jnp.sum(e, axis=-1, keepdims=True)
        lses.append((m + jnp.log(den))[..., 0])
        outs.append(jnp.einsum('bhqj,bqjhd->bqhd', (e / den).astype(vg.dtype), vg))
    wg = jax.nn.softmax(jnp.stack(lses), axis=0)
    o = jnp.einsum('gbhq,gbqhd->bqhd', wg, jnp.stack(outs).astype(jnp.float32))
    return o.astype(q.dtype)


def _mem_kv(mem, g, w):
    b, n = mem.shape[0], mem.shape[1]
    k, v = jnp.split(_rms(mem, g) @ w, 2, axis=-1)
    return k.reshape(b, n, MEM_HEADS, MEM_HEAD_DIM), v.reshape(b, n, MEM_HEADS, MEM_HEAD_DIM)


def _mem_attend(h, k_m, v_m, w_q, w_o):
    b, t = h.shape[0], h.shape[1]
    q = (h @ w_q).reshape(b, t, MEM_HEADS, MEM_HEAD_DIM)
    s = jnp.einsum('bqhd,bkhd->bhqk', q, k_m).astype(jnp.float32) * (MEM_HEAD_DIM ** -0.5)
    p = jax.nn.softmax(s, axis=-1)
    o = jnp.einsum('bhqk,bkhd->bqhd', p.astype(v_m.dtype), v_m).reshape(b, t, D_MODEL)
    return o @ w_o


def _swiglu(h, w_gu, w_d):
    g, u = jnp.split(h @ w_gu, 2, axis=-1)
    return (jax.nn.silu(g) * u) @ w_d


def setup_inputs(seed: int = 0) -> dict:
    key = jax.random.key(seed)
    ks = jax.random.split(key, 24)
    n_pages = PAST_LEN // PAGE_SIZE
    used = DEC_BATCH * n_pages
    n_phys = used + (used + 3) // 4
    cwin = min(MAX_WINDOW, PAST_LEN)
    f32 = jnp.float32
    nrm = lambda k, shp, sc=1.0: jax.random.normal(k, shp, f32) * sc
    gain = lambda k, shp: 1.0 + 0.02 * jax.random.normal(k, shp, f32)
    page_table = jax.random.permutation(ks[0], n_phys)[:used].reshape(DEC_BATCH, n_pages).astype(jnp.int32)
    return {
        'x_prompt': nrm(ks[1], (BATCH, SEQ, D_MODEL)),
        'x_sample': nrm(ks[2], (DEC_BATCH, DEC_SEQ, D_MODEL)),
        'cache_k_even': nrm(ks[3], (N_EVEN, n_phys, PAGE_SIZE, N_HEADS, HEAD_DIM)),
        'cache_v_even': nrm(ks[4], (N_EVEN, n_phys, PAGE_SIZE, N_HEADS, HEAD_DIM)),
        'cache_k_odd': nrm(ks[5], (N_ODD, DEC_BATCH, cwin, N_HEADS, HEAD_DIM)),
        'cache_v_odd': nrm(ks[6], (N_ODD, DEC_BATCH, cwin, N_HEADS, HEAD_DIM)),
        'cache_mem_k': nrm(ks[7], (DEPTH, DEC_BATCH, N_MEM, MEM_HEADS, MEM_HEAD_DIM)),
        'cache_mem_v': nrm(ks[8], (DEPTH, DEC_BATCH, N_MEM, MEM_HEADS, MEM_HEAD_DIM)),
        'page_table': page_table,
        'mem_prompt': nrm(ks[9], (BATCH, N_MEM, D_MODEL)),
        'w_in': nrm(ks[10], (DEPTH, D_MODEL, 3 * D_MODEL), D_MODEL ** -0.5),
        'w_out': nrm(ks[11], (DEPTH, D_MODEL, D_MODEL), D_MODEL ** -0.5),
        'diff_lambda': nrm(ks[12], (N_EVEN, 4, HEAD_DIM), 0.1),
        'diff_subln': gain(ks[13], (N_EVEN, 2 * HEAD_DIM)),
        'norm_mix': gain(ks[14], (DEPTH, D_MODEL)),
        'norm_mem_q': gain(ks[15], (DEPTH, D_MODEL)),
        'norm_mem_kv': gain(ks[16], (DEPTH, D_MODEL)),
        'w_q_mem': nrm(ks[17], (DEPTH, D_MODEL, D_MODEL), D_MODEL ** -0.5),
        'w_kv_mem': nrm(ks[18], (DEPTH, D_MODEL, 2 * D_MODEL), D_MODEL ** -0.5),
        'w_o_mem': nrm(ks[19], (DEPTH, D_MODEL, D_MODEL), D_MODEL ** -0.5),
        'norm_ffn': gain(ks[20], (DEPTH, D_MODEL)),
        'w_gate_up': nrm(ks[21], (DEPTH, D_MODEL, 2 * D_FF), D_MODEL ** -0.5),
        'w_down': nrm(ks[22], (DEPTH, D_FF, D_MODEL), D_FF ** -0.5),
        'norm_final': gain(ks[23], (D_MODEL,)),
    }


def reference(x_prompt, x_sample, cache_k_even, cache_v_even, cache_k_odd, cache_v_odd,
              cache_mem_k, cache_mem_v, page_table, mem_prompt, w_in, w_out, diff_lambda,
              diff_subln, norm_mix, norm_mem_q, norm_mem_kv, w_q_mem, w_kv_mem, w_o_mem,
              norm_ffn, w_gate_up, w_down, norm_final):
    b_p, t_p = x_prompt.shape[0], x_prompt.shape[1]
    b_s, t_s = x_sample.shape[0], x_sample.shape[1]
    pos_p = jnp.arange(t_p, dtype=jnp.int32)
    pos_s = PAST_LEN + jnp.arange(t_s, dtype=jnp.int32)
    hp, hs = x_prompt, x_sample
    kep, vep, kes, ves, kop, vop, kos, vos, mkp, mvp = ([] for _ in range(10))
    for l in range(DEPTH):
        if l % 2 == 0:
            e = l // 2
            lam_init = 0.8 - 0.6 * math.exp(-0.3 * l)
            lp = diff_lambda[e].astype(jnp.float32)
            lam = jnp.exp(jnp.sum(lp[0] * lp[1])) - jnp.exp(jnp.sum(lp[2] * lp[3])) + lam_init
            q, k, v = _even_qkv(_rms(hp, norm_mix[l]), w_in[l], pos_p)
            oa, ob = _sweep_blocks(lambda qb, pb: _ab_attend(qb, pb, [(k, v)], pos_p, lam), q, pos_p)
            hp = hp + _ab_merge(oa, ob, diff_subln[e], lam_init, w_out[l])
            kep.append(k)
            vep.append(v)
            qs, ks_, vs_ = _even_qkv(_rms(hs, norm_mix[l]), w_in[l], pos_s)
            k_past = cache_k_even[e, page_table].reshape(b_s, -1, N_HEADS, HEAD_DIM)
            v_past = cache_v_even[e, page_table].reshape(b_s, -1, N_HEADS, HEAD_DIM)
            kpos_s = jnp.arange(k_past.shape[1] + t_s, dtype=jnp.int32)
            oa, ob = _sweep_blocks(
                lambda qb, pb: _ab_attend(qb, pb, [(k_past, v_past), (ks_, vs_)], kpos_s, lam), qs, pos_s)
            hs = hs + _ab_merge(oa, ob, diff_subln[e], lam_init, w_out[l])
            kes.append(ks_)
            ves.append(vs_)
        else:
            o_idx = l // 2
            q, k, v = _odd_qkv(_rms(hp, norm_mix[l]), w_in[l], pos_p)
            y = _sweep_blocks(lambda qb, pb: _dilated_attend(qb, pb, [(k, 0)], [(v, 0)]), q, pos_p)
            hp = hp + y.reshape(b_p, t_p, D_MODEL) @ w_out[l]
            keep = min(MAX_WINDOW, t_p)
            kop.append(k[:, t_p - keep:])
            vop.append(v[:, t_p - keep:])
            qs, ks_, vs_ = _odd_qkv(_rms(hs, norm_mix[l]), w_in[l], pos_s)
            buf_k = cache_k_odd[o_idx]
            buf_v = cache_v_odd[o_idx]
            base = PAST_LEN - buf_k.shape[1]
            y = _sweep_blocks(
                lambda qb, pb: _dilated_attend(qb, pb, [(buf_k, base), (ks_, PAST_LEN)],
                                               [(buf_v, base), (vs_, PAST_LEN)]), qs, pos_s)
            hs = hs + y.reshape(b_s, t_s, D_MODEL) @ w_out[l]
            kos.append(ks_)
            vos.append(vs_)
        mk, mv = _mem_kv(mem_prompt, norm_mem_kv[l], w_kv_mem[l])
        hp = hp + _mem_attend(_rms(hp, norm_mem_q[l]), mk, mv, w_q_mem[l], w_o_mem[l])
        hs = hs + _mem_attend(_rms(hs, norm_mem_q[l]), cache_mem_k[l], cache_mem_v[l], w_q_mem[l], w_o_mem[l])
        mkp.append(mk)
        mvp.append(mv)
        hp = hp + _swiglu(_rms(hp, norm_ffn[l]), w_gate_up[l], w_down[l])
        hs = hs + _swiglu(_rms(hs, norm_ffn[l]), w_gate_up[l], w_down[l])
    y_prompt = _rms(hp, norm_final)
    y_sample = _rms(hs, norm_final)
    return (y_prompt, y_sample,
            jnp.stack(kep), jnp.stack(vep), jnp.stack(kes), jnp.stack(ves),
            jnp.stack(kop), jnp.stack(vop), jnp.stack(kos), jnp.stack(vos),
            jnp.stack(mkp), jnp.stack(mvp))
```

```python
import functools
import math

import jax
import jax.numpy as jnp
from jax import lax
from jax.experimental import pallas as pl
from jax.experimental.pallas import tpu as pltpu

F32 = jnp.float32
BF16 = jnp.bfloat16

HEAD_DIM = 64
ROT_DIM = HEAD_DIM // 4
ROPE_THETA = 500000.0
DILATED_PATTERNS = ((128, 1), (512, 4), (2048, 16))
MAX_WINDOW = 2048
MEM_HEADS = 4
RMS_EPS = 1e-6
NEG_INF = -1e30

LANES = 128
VMEM_LIMIT = 56 * 1024 * 1024

_NT = (((1,), (1,)), ((), ()))


def _cparams(sem):
    return pltpu.CompilerParams(dimension_semantics=sem, vmem_limit_bytes=VMEM_LIMIT)


def _dot(a, b):
    return jnp.dot(a, b, preferred_element_type=F32)


def _dot_nt(a, b):
    return lax.dot_general(a, b, _NT, preferred_element_type=F32)


def _rms_rows(x, g):
    return x * lax.rsqrt(jnp.mean(x * x, axis=-1, keepdims=True) + RMS_EPS) * g


def _resident(shape):
    nd = len(shape)
    return pl.BlockSpec(shape, lambda *_: (0,) * nd, pipeline_mode=pl.Buffered(1))


def _proj_kernel(x_ref, g_ref, w_ref, c_ref, s1_ref, s2_ref,
                 q_ref, kf_ref, vf_ref, kb_ref, vb_ref, *, d, rope_cols):
    hn = _rms_rows(x_ref[...], g_ref[...]).astype(BF16)
    cos, s_lo, s_hi = c_ref[...], s1_ref[...], s2_ref[...]

    def rope(xc):
        return xc * cos + pltpu.roll(xc, 8, 1) * s_lo + pltpu.roll(xc, LANES - 8, 1) * s_hi

    q = _dot(hn, w_ref[:, 0:d])
    k = _dot(hn, w_ref[:, d:2 * d])
    v = _dot(hn, w_ref[:, 2 * d:3 * d])
    for c in range(d // LANES):
        sl = slice(c * LANES, (c + 1) * LANES)
        qc, kc = q[:, sl], k[:, sl]
        if c * LANES < rope_cols:
            qc, kc = rope(qc), rope(kc)
        q_ref[:, sl] = (qc * (HEAD_DIM ** -0.5)).astype(BF16)
        kf_ref[:, sl] = kc
        kb_ref[:, sl] = kc.astype(BF16)
    vf_ref[...] = v
    vb_ref[...] = v.astype(BF16)


def _proj(h, g, w_bf, rope_tabs, rope_cols, tm):
    r, d = h.shape
    row = lambda i: (i, 0)
    tab = pl.BlockSpec((tm, LANES), row)
    blk = pl.BlockSpec((tm, d), row)
    return pl.pallas_call(
        functools.partial(_proj_kernel, d=d, rope_cols=rope_cols),
        grid=(r // tm,),
        in_specs=[blk, _resident((1, d)), _resident((d, 3 * d)), tab, tab, tab],
        out_specs=[blk] * 5,
        out_shape=[jax.ShapeDtypeStruct((r, d), t) for t in (BF16, F32, F32, BF16, BF16)],
        compiler_params=_cparams(("parallel",)),
        name="qkv_proj",
    )(h, g, w_bf, *rope_tabs)


def _rms_mm_kernel(x_ref, g_ref, w_ref, o_ref):
    hn = _rms_rows(x_ref[...], g_ref[...]).astype(BF16)
    o_ref[...] = _dot(hn, w_ref[...]).astype(o_ref.dtype)


def _rms_mm(h, g, w_bf, out_dtype, tm, name):
    r, d = h.shape
    n = w_bf.shape[1]
    return pl.pallas_call(
        _rms_mm_kernel,
        grid=(r // tm,),
        in_specs=[pl.BlockSpec((tm, d), lambda i: (i, 0)), _resident((1, d)), _resident((d, n))],
        out_specs=pl.BlockSpec((tm, n), lambda i: (i, 0)),
        out_shape=jax.ShapeDtypeStruct((r, n), out_dtype),
        compiler_params=_cparams(("parallel",)),
        name=name,
    )(h, g, w_bf)


def _mm_res_kernel(a_ref, w_ref, r_ref, o_ref):
    o_ref[...] = r_ref[...] + _dot(a_ref[...], w_ref[...])


def _mm_res(a_bf, w_bf, res, tm, name):
    r, k = a_bf.shape
    n = w_bf.shape[1]
    return pl.pallas_call(
        _mm_res_kernel,
        grid=(r // tm,),
        in_specs=[pl.BlockSpec((tm, k), lambda i: (i, 0)), _resident((k, n)),
                  pl.BlockSpec((tm, n), lambda i: (i, 0))],
        out_specs=pl.BlockSpec((tm, n), lambda i: (i, 0)),
        out_shape=jax.ShapeDtypeStruct((r, n), F32),
        compiler_params=_cparams(("parallel",)),
        name=name,
    )(a_bf, w_bf, res)


def _ffn_kernel(x_ref, g_ref, wg_ref, wu_ref, wd_ref, o_ref, *, f_chunk):
    x = x_ref[...]
    hn = _rms_rows(x, g_ref[...]).astype(BF16)
    acc = x
    for c in range(wg_ref.shape[1] // f_chunk):
        sl = slice(c * f_chunk, (c + 1) * f_chunk)
        gate = _dot(hn, wg_ref[:, sl])
        up = _dot(hn, wu_ref[:, sl])
        act = (gate / (1.0 + jnp.exp(-gate)) * up).astype(BF16)
        acc = acc + _dot(act, wd_ref[sl, :])
    o_ref[...] = acc


def _ffn(h, g, wg_bf, wu_bf, wd_bf, tm):
    r, d = h.shape
    f = wg_bf.shape[1]
    f_chunk = 256 if f % 256 == 0 else f
    return pl.pallas_call(
        functools.partial(_ffn_kernel, f_chunk=f_chunk),
        grid=(r // tm,),
        in_specs=[pl.BlockSpec((tm, d), lambda i: (i, 0)), _resident((1, d)),
                  _resident((d, f)), _resident((d, f)), _resident((f, d))],
        out_specs=pl.BlockSpec((tm, d), lambda i: (i, 0)),
        out_shape=jax.ShapeDtypeStruct((r, d), F32),
        compiler_params=_cparams(("parallel",)),
        name="swiglu",
    )(h, g, wg_bf, wu_bf, wd_bf)


def _rms_kernel(x_ref, g_ref, o_ref):
    o_ref[...] = _rms_rows(x_ref[...], g_ref[...])


def _rms(h, g, tm):
    r, d = h.shape
    return pl.pallas_call(
        _rms_kernel,
        grid=(r // tm,),
        in_specs=[pl.BlockSpec((tm, d), lambda i: (i, 0)), _resident((1, d))],
        out_specs=pl.BlockSpec((tm, d), lambda i: (i, 0)),
        out_shape=jax.ShapeDtypeStruct((r, d), F32),
        compiler_params=_cparams(("parallel",)),
        name="final_rms",
    )(h, g)


def _stack_slots(q):
    lane = lax.broadcasted_iota(jnp.int32, (1, LANES), 1)
    zero = jnp.zeros_like(q)
    return jnp.concatenate([jnp.where(lane < HEAD_DIM, q, zero),
                            jnp.where(lane >= HEAD_DIM, q, zero)], axis=0)


def _merge_slots(acc, tq):
    lane = lax.broadcasted_iota(jnp.int32, (1, LANES), 1)
    return jnp.where(lane < HEAD_DIM, acc[:tq], acc[tq:])


def _both_slots(fn, s, tq):
    return fn(s.reshape(2, tq, s.shape[-1])).reshape(s.shape)


def _kv_block(ref, j, tk):
    return ref[pl.ds(pl.multiple_of(j * tk, tk), tk), :]


def _diff_kernel(lam_ref, g_ref, q_ref, k_ref, v_ref, o_ref, m_sc, l_sc, acc_sc, *, tq, tk, out_scale):
    i = pl.program_id(2)
    qq = _stack_slots(q_ref[...])
    jd = (i * tq) // tk

    s = _dot_nt(qq, _kv_block(k_ref, jd, tk))
    qpos = i * tq + lax.broadcasted_iota(jnp.int32, (tq, tk), 0)
    kpos = jd * tk + lax.broadcasted_iota(jnp.int32, (tq, tk), 1)
    s = _both_slots(lambda x: jnp.where((kpos <= qpos)[None], x, NEG_INF), s, tq)
    m = jnp.max(s, axis=-1, keepdims=True)
    p = jnp.exp(s - m)
    m_sc[...] = m
    l_sc[...] = jnp.sum(p, axis=-1, keepdims=True)
    acc_sc[...] = _dot(p.astype(BF16), _kv_block(v_ref, jd, tk))

    def body(j, carry):
        s = _dot_nt(qq, _kv_block(k_ref, j, tk))
        m_old = m_sc[...]
        m_new = jnp.maximum(m_old, jnp.max(s, axis=-1, keepdims=True))
        alpha = jnp.exp(m_old - m_new)
        p = jnp.exp(s - m_new)
        l_sc[...] = alpha * l_sc[...] + jnp.sum(p, axis=-1, keepdims=True)
        acc_sc[...] = alpha * acc_sc[...] + _dot(p.astype(BF16), _kv_block(v_ref, j, tk))
        m_sc[...] = m_new
        return carry

    lax.fori_loop(0, jd, body, 0)

    a = acc_sc[...] / l_sc[...]
    o = a[:tq] - lam_ref[...] * a[tq:]
    o_ref[...] = (_rms_rows(o, g_ref[...]) * out_scale).astype(o_ref.dtype)


def _softplus(z):
    return jnp.maximum(z, 0.0) + jnp.log(1.0 + jnp.exp(-jnp.abs(z)))


def _stick_weights(z, tri, carry):
    lstay = -_softplus(z)
    hi = lstay.astype(BF16)
    lo = (lstay - hi.astype(F32)).astype(BF16)
    c = _dot(hi, tri) + _dot(lo, tri) + carry
    return jnp.exp(z + c), c[:, 0:1]


def _stick_kernel(tri_ref, q_ref, k_ref, v_ref, o_ref, c_sc, acc_sc, *, tq, tk):
    i = pl.program_id(2)
    qq = _stack_slots(q_ref[...])
    tri = tri_ref[...]
    jd = (i * tq) // tk

    z = _dot_nt(qq, _kv_block(k_ref, jd, tk))
    qpos = i * tq + lax.broadcasted_iota(jnp.int32, (tq, tk), 0)
    kpos = jd * tk + lax.broadcasted_iota(jnp.int32, (tq, tk), 1)
    z = _both_slots(lambda x: jnp.where((kpos < qpos)[None], x, NEG_INF), z, tq)
    w, c = _stick_weights(z, tri, 0.0)
    c_sc[...] = c
    acc_sc[...] = _dot(w.astype(BF16), _kv_block(v_ref, jd, tk))

    def body(t, carry):
        j = jd - 1 - t
        z = _dot_nt(qq, _kv_block(k_ref, j, tk))
        w, c = _stick_weights(z, tri, c_sc[...])
        c_sc[...] = c
        acc_sc[...] += _dot(w.astype(BF16), _kv_block(v_ref, j, tk))
        return carry

    lax.fori_loop(0, jd, body, 0)
    o_ref[...] = _merge_slots(acc_sc[...], tq).astype(o_ref.dtype)


def _window_kernel(bias_ref, q_ref, k_ref, v_ref, o_ref, m_sc, l_sc, acc_sc, *, tq, n_back):
    i = pl.program_id(2)
    qq = _stack_slots(q_ref[...])

    def scores(j):
        b = bias_ref[i - j]
        return _both_slots(lambda x: x + b[None], _dot_nt(qq, _kv_block(k_ref, j, tq)), tq)

    s = scores(i)
    m = jnp.max(s, axis=-1, keepdims=True)
    p = jnp.exp(s - m)
    m_sc[...] = m
    l_sc[...] = jnp.sum(p, axis=-1, keepdims=True)
    acc_sc[...] = _dot(p.astype(BF16), _kv_block(v_ref, i, tq))

    def body(t, carry):
        j = i - 1 - t
        s = scores(j)
        m_old = m_sc[...]
        m_new = jnp.maximum(m_old, jnp.max(s, axis=-1, keepdims=True))
        alpha = jnp.exp(m_old - m_new)
        p = jnp.exp(s - m_new)
        l_sc[...] = alpha * l_sc[...] + jnp.sum(p, axis=-1, keepdims=True)
        acc_sc[...] = alpha * acc_sc[...] + _dot(p.astype(BF16), _kv_block(v_ref, j, tq))
        m_sc[...] = m_new
        return carry

    lax.fori_loop(0, jnp.minimum(i, n_back), body, 0)
    o_ref[...] = _merge_slots(acc_sc[...] / l_sc[...], tq).astype(o_ref.dtype)


def _prompt_attn_specs(b, t, tq, col0):
    nq = t // tq
    qspec = pl.BlockSpec((tq, LANES), lambda bi, p, i: (bi * nq + i, col0 + p))
    kvspec = pl.BlockSpec((t, LANES), lambda bi, p, i: (bi, col0 + p))
    ospec = pl.BlockSpec((tq, LANES), lambda bi, p, i: (bi * nq + i, p))
    return nq, qspec, kvspec, ospec


def _diff_attn(q, k, v, lam_vec, g_vec, b, t, n_pairs, out_scale, tq, tk):
    nq, qspec, kvspec, ospec = _prompt_attn_specs(b, t, tq, 0)
    return pl.pallas_call(
        functools.partial(_diff_kernel, tq=tq, tk=tk, out_scale=out_scale),
        grid=(b, n_pairs, nq),
        in_specs=[_resident((1, LANES)), _resident((1, LANES)), qspec, kvspec, kvspec],
        out_specs=ospec,
        out_shape=jax.ShapeDtypeStruct((b * t, n_pairs * LANES), BF16),
        scratch_shapes=[pltpu.VMEM((2 * tq, 1), F32), pltpu.VMEM((2 * tq, 1), F32),
                        pltpu.VMEM((2 * tq, LANES), F32)],
        compiler_params=_cparams(("parallel", "parallel", "arbitrary")),
        name="diff_attn",
    )(lam_vec, g_vec, q, k, v)


def _stick_attn(q, k, v, tri, b, t, n_pairs, col0, tq, tk):
    nq, qspec, kvspec, ospec = _prompt_attn_specs(b, t, tq, col0)
    return pl.pallas_call(
        functools.partial(_stick_kernel, tq=tq, tk=tk),
        grid=(b, n_pairs, nq),
        in_specs=[_resident((tk, tk)), qspec, kvspec, kvspec],
        out_specs=ospec,
        out_shape=jax.ShapeDtypeStruct((b * t, n_pairs * LANES), BF16),
        scratch_shapes=[pltpu.VMEM((2 * tq, 1), F32), pltpu.VMEM((2 * tq, LANES), F32)],
        compiler_params=_cparams(("parallel", "parallel", "arbitrary")),
        name="stick_attn",
    )(tri, q, k, v)


def _window_attn(q, k, v, bias, b, t, n_pairs, tq):
    nq, qspec, kvspec, ospec = _prompt_attn_specs(b, t, tq, 0)
    return pl.pallas_call(
        functools.partial(_window_kernel, tq=tq, n_back=bias.shape[0] - 1),
        grid=(b, n_pairs, nq),
        in_specs=[_resident(bias.shape), qspec, kvspec, kvspec],
        out_specs=ospec,
        out_shape=jax.ShapeDtypeStruct((b * t, n_pairs * LANES), BF16),
        scratch_shapes=[pltpu.VMEM((2 * tq, 1), F32), pltpu.VMEM((2 * tq, 1), F32),
                        pltpu.VMEM((2 * tq, LANES), F32)],
        compiler_params=_cparams(("parallel", "parallel", "arbitrary")),
        name="window_attn",
    )(bias, q, k, v)


def _log_multiplicity(dist):
    cnt = jnp.zeros(dist.shape, F32)
    for window, dil in DILATED_PATTERNS:
        cnt = cnt + ((dist >= 0) & (dist <= window) & (dist % dil == 0)).astype(F32)
    return jnp.where(cnt > 0, jnp.log(jnp.maximum(cnt, 1.0)), NEG_INF)


def _slot_mask(n_rows, d, first_slot, width):
    r = lax.broadcasted_iota(jnp.int32, (n_rows, d), 0) + first_slot
    c = lax.broadcasted_iota(jnp.int32, (n_rows, d), 1)
    return jnp.right_shift(c, width.bit_length() - 1) == r


def _sample_kernel(*refs, even, ts, d, out_scale):
    if even:
        (_, lam_ref, g_ref, tri_ref, q_ref, bias_ref, kn_ref, vn_ref, kc_ref, vc_ref,
         o_ref, qbd_sc, m_sc, l_sc, acc_sc, c_sc) = refs
        n_soft = ts * 8
    else:
        (q_ref, bias_ref, kn_ref, vn_ref, kc_ref, vc_ref,
         o_ref, qbd_sc, m_sc, l_sc, acc_sc) = refs
        n_soft = ts * 16
    c = pl.program_id(1)
    n_rows = ts * 16

    def process(kc, vc, first):
        s = _dot_nt(qbd_sc[...], kc) + bias_ref[0]
        ss = s[:n_soft]
        if first:
            m_new = jnp.max(ss, axis=-1, keepdims=True)
            p = jnp.exp(ss - m_new)
            l_sc[...] = jnp.sum(p, axis=-1, keepdims=True)
        else:
            m_old = m_sc[...]
            m_new = jnp.maximum(m_old, jnp.max(ss, axis=-1, keepdims=True))
            alpha = jnp.exp(m_old - m_new)
            p = jnp.exp(ss - m_new)
            l_sc[...] = alpha * l_sc[...] + jnp.sum(p, axis=-1, keepdims=True)
        m_sc[...] = m_new
        if even:
            w, cnew = _stick_weights(s[n_soft:], tri_ref[...], 0.0 if first else c_sc[...])
            c_sc[...] = cnew
            p = jnp.concatenate([p, w], axis=0)
        pv = _dot(p.astype(BF16), vc)
        if first:
            acc_sc[...] = pv
        else:
            acc_sc[0:n_soft] = alpha * acc_sc[0:n_soft] + pv[:n_soft]
            if even:
                acc_sc[n_soft:n_rows] = acc_sc[n_soft:n_rows] + pv[n_soft:]

    @pl.when(c == 0)
    def _():
        q = q_ref[0]
        if even:
            md = _slot_mask(8, d, 0, HEAD_DIM)
            mb = _slot_mask(8, d, 8, HEAD_DIM)
            rows = [jnp.where(md, jnp.broadcast_to(q[t:t + 1], (8, d)), 0.0) for t in range(ts)]
            rows += [jnp.where(mb, jnp.broadcast_to(q[t:t + 1], (8, d)), 0.0) for t in range(ts)]
        else:
            ma = _slot_mask(16, d, 0, HEAD_DIM)
            rows = [jnp.where(ma, jnp.broadcast_to(q[t:t + 1], (16, d)), 0.0) for t in range(ts)]
        qbd_sc[...] = jnp.concatenate(rows, axis=0).astype(BF16)
        process(kn_ref[0], vn_ref[0], True)

    @pl.when(c > 0)
    def _():
        process(kc_ref[...].astype(BF16), vc_ref[...].astype(BF16), False)

    @pl.when(c == pl.num_programs(1) - 1)
    def _():
        acc = acc_sc[...]
        if even:
            a = (acc[:n_soft] / l_sc[...]).reshape(ts, 8, d)
            r = lax.broadcasted_iota(jnp.int32, (8, d), 0)
            col = lax.broadcasted_iota(jnp.int32, (8, d), 1)
            head2 = 2 * jnp.right_shift(col, (2 * HEAD_DIM).bit_length() - 1)
            pick0 = (r == head2)[None]
            pick1 = (r == head2 + 1)[None]
            o0 = jnp.sum(jnp.where(pick0, a, 0.0), axis=1)
            o1 = jnp.sum(jnp.where(pick1, a, 0.0), axis=1)
            ob = jnp.sum(jnp.where(_slot_mask(8, d, 8, HEAD_DIM)[None],
                                   acc[n_soft:].reshape(ts, 8, d), 0.0), axis=1)
            half = d // 2
            for hd in range(half // LANES):
                sl = slice(hd * LANES, (hd + 1) * LANES)
                oa = o0[:, sl] - lam_ref[...] * o1[:, sl]
                o_ref[0, :, sl] = (_rms_rows(oa, g_ref[...]) * out_scale).astype(o_ref.dtype)
            o_ref[0, :, half:] = ob[:, half:].astype(o_ref.dtype)
        else:
            a = (acc / l_sc[...]).reshape(ts, 16, d)
            o = jnp.sum(jnp.where(_slot_mask(16, d, 0, HEAD_DIM)[None], a, 0.0), axis=1)
            o_ref[0] = o.astype(o_ref.dtype)


def _sample_attn(q3, bias, k_new, v_new, cache_k, cache_v, layer, *, even, page_table=None,
                 lam_vec=None, g_vec=None, tri=None, out_scale=1.0):
    s, ts, d = q3.shape
    n_steps = bias.shape[0]
    n_chunks = n_steps - 1
    n_rows = ts * 16
    chunk = lambda c: n_chunks - jnp.maximum(c, 1)
    if even:
        cache_map = lambda si, c, pt: (layer, pt[si, chunk(c)], 0, 0)
        seq3 = lambda si, c, pt: (si, 0, 0)
        step3 = lambda si, c, pt: (c, 0, 0)
        const2 = lambda si, c, pt: (0, 0)
    else:
        cache_map = lambda si, c: (layer, si, chunk(c), 0)
        seq3 = lambda si, c: (si, 0, 0)
        step3 = lambda si, c: (c, 0, 0)
    cache_spec = pl.BlockSpec((None, None, LANES, d), cache_map)
    in_specs = [pl.BlockSpec((1, ts, d), seq3), pl.BlockSpec((1, n_rows, LANES), step3),
                pl.BlockSpec((1, LANES, d), seq3), pl.BlockSpec((1, LANES, d), seq3),
                cache_spec, cache_spec]
    args = [q3, bias, k_new, v_new, cache_k, cache_v]
    scratch = [pltpu.VMEM((n_rows, d), BF16),
               pltpu.VMEM((ts * 8 if even else n_rows, 1), F32),
               pltpu.VMEM((ts * 8 if even else n_rows, 1), F32),
               pltpu.VMEM((n_rows, d), F32)]
    if even:
        in_specs = [pl.BlockSpec((1, LANES), const2), pl.BlockSpec((1, LANES), const2),
                    pl.BlockSpec((LANES, LANES), const2)] + in_specs
        args = [page_table, lam_vec, g_vec, tri] + args
        scratch.append(pltpu.VMEM((ts * 8, 1), F32))
    grid_spec = pltpu.PrefetchScalarGridSpec(
        num_scalar_prefetch=1 if even else 0,
        grid=(s, n_steps),
        in_specs=in_specs,
        out_specs=pl.BlockSpec((1, ts, d), seq3),
        scratch_shapes=scratch,
    )
    return pl.pallas_call(
        functools.partial(_sample_kernel, even=even, ts=ts, d=d, out_scale=out_scale),
        grid_spec=grid_spec,
        out_shape=jax.ShapeDtypeStruct((s, ts, d), F32),
        compiler_params=_cparams(("parallel", "arbitrary")),
        name="sample_attn_even" if even else "sample_attn_odd",
    )(*args)


def _mem_heads(q, k, v, o_store):
    d = q.shape[-1]
    hd = d // MEM_HEADS
    for h in range(MEM_HEADS):
        sl = slice(h * hd, (h + 1) * hd)
        s = _dot_nt(q[:, sl].astype(BF16), k[:, sl]) * (hd ** -0.5)
        p = jnp.exp(s - jnp.max(s, axis=-1, keepdims=True))
        o = _dot(p.astype(BF16), v[:, sl]) / jnp.sum(p, axis=-1, keepdims=True)
        o_store(sl, o)


def _mem_prompt_kernel(q_ref, k_ref, v_ref, o_ref):
    def store(sl, o):
        o_ref[:, sl] = o.astype(o_ref.dtype)
    _mem_heads(q_ref[...], k_ref[...], v_ref[...], store)


def _mem_prompt(q, mk_bf, mv_bf, b, t, tq):
    d = q.shape[1]
    n_mem = mk_bf.shape[0] // b
    nq = t // tq
    return pl.pallas_call(
        _mem_prompt_kernel,
        grid=(b, nq),
        in_specs=[pl.BlockSpec((tq, d), lambda bi, i: (bi * nq + i, 0)),
                  pl.BlockSpec((n_mem, d), lambda bi, i: (bi, 0)),
                  pl.BlockSpec((n_mem, d), lambda bi, i: (bi, 0))],
        out_specs=pl.BlockSpec((tq, d), lambda bi, i: (bi * nq + i, 0)),
        out_shape=jax.ShapeDtypeStruct((b * t, d), BF16),
        compiler_params=_cparams(("parallel", "parallel")),
        name="mem_attn_prompt",
    )(q, mk_bf, mv_bf)


def _mem_sample_kernel(q_ref, k_ref, v_ref, o_ref, *, n_seq):
    for si in range(n_seq):
        def store(sl, o, si=si):
            o_ref[si, :, sl] = o.astype(o_ref.dtype)
        _mem_heads(q_ref[si], k_ref[si].astype(BF16), v_ref[si].astype(BF16), store)


def _mem_sample(q3, cache_k, cache_v, layer, n_seq):
    s, ts, d = q3.shape
    n_mem = cache_k.shape[2]
    kv = pl.BlockSpec((None, n_seq, n_mem, d), lambda i: (layer, i, 0, 0))
    return pl.pallas_call(
        functools.partial(_mem_sample_kernel, n_seq=n_seq),
        grid=(s // n_seq,),
        in_specs=[pl.BlockSpec((n_seq, ts, d), lambda i: (i, 0, 0)), kv, kv],
        out_specs=pl.BlockSpec((n_seq, ts, d), lambda i: (i, 0, 0)),
        out_shape=jax.ShapeDtypeStruct((s, ts, d), F32),
        compiler_params=_cparams(("parallel",)),
        name="mem_attn_sample",
    )(q3, cache_k, cache_v)


def _rope_tables(pos):
    half = ROT_DIM // 2
    inv = jnp.power(jnp.float32(ROPE_THETA), -jnp.arange(half, dtype=F32) / half)
    ang = pos.astype(F32)[:, None] * inv[None, :]
    cos, sin = jnp.cos(ang), jnp.sin(ang)
    n = pos.shape[0]
    pad = jnp.zeros((n, HEAD_DIM - ROT_DIM), F32)
    z = jnp.zeros((n, half), F32)
    c_head = jnp.concatenate([cos, cos, pad + 1.0], axis=1)
    lo_head = jnp.concatenate([z, sin, pad], axis=1)
    hi_head = jnp.concatenate([-sin, z, pad], axis=1)
    rep = LANES // HEAD_DIM
    return tuple(jnp.tile(a, (1, rep)) for a in (c_head, lo_head, hi_head))


def kernel(x_prompt, x_sample, cache_k_even, cache_v_even, cache_k_odd, cache_v_odd, cache_mem_k, cache_mem_v, page_table, mem_prompt, w_in, w_out, diff_lambda, diff_subln, norm_mix, norm_mem_q, norm_mem_kv, w_q_mem, w_kv_mem, w_o_mem, norm_ffn, w_gate_up, w_down, norm_final):
    b, t, d = x_prompt.shape
    s, ts, _ = x_sample.shape
    depth = w_in.shape[0]
    n_heads = d // HEAD_DIM
    n_pages, page = page_table.shape[1], cache_k_even.shape[2]
    past = n_pages * page
    cwin = cache_k_odd.shape[2]
    n_mem = mem_prompt.shape[1]
    d_ff = w_down.shape[1]
    assert d % (2 * LANES) == 0 and page == LANES and cwin % LANES == 0 and past == cwin

    rp, rs = b * t, s * ts
    tm = 512
    r_pad = -(-(rp + rs) // tm) * tm
    tq = min(256, t)
    tk = min(512, t)
    assert t % tq == 0 and t % tk == 0 and tk % tq == 0 and rp % tm == 0

    h = jnp.concatenate([x_prompt.reshape(rp, d), x_sample.reshape(rs, d),
                         jnp.zeros((r_pad - rp - rs, d), F32)], axis=0)
    pos = jnp.concatenate([jnp.tile(jnp.arange(t, dtype=jnp.int32), b),
                           jnp.tile(past + jnp.arange(ts, dtype=jnp.int32), s),
                           jnp.zeros((r_pad - rp - rs,), jnp.int32)])
    rope_tabs = _rope_tables(pos)

    tri_p = jnp.tril(jnp.ones((tk, tk), F32)).astype(BF16)
    tri_s = jnp.tril(jnp.ones((LANES, LANES), F32)).astype(BF16)
    n_back = -(-MAX_WINDOW // tq)
    dist = (jnp.arange(n_back + 1, dtype=jnp.int32)[:, None, None] * tq
            + jnp.arange(tq, dtype=jnp.int32)[None, :, None] - jnp.arange(tq, dtype=jnp.int32)[None, None, :])
    bias_window = _log_multiplicity(dist)

    n_chunks = past // LANES
    qi = jnp.arange(ts, dtype=jnp.int32)
    key = jnp.arange(LANES, dtype=jnp.int32)
    new_valid = key[None, :] < ts
    causal_new = jnp.where(new_valid & (key[None, :] <= qi[:, None]), 0.0, NEG_INF)
    strict_new = jnp.where(new_valid & (key[None, :] < qi[:, None]), 0.0, NEG_INF)
    rows8 = lambda a: jnp.repeat(a, 8, axis=0)
    bias_even = jnp.concatenate(
        [jnp.concatenate([rows8(causal_new), rows8(strict_new)], axis=0)[None],
         jnp.zeros((n_chunks, ts * 16, LANES), F32)], axis=0).astype(F32)
    new_dist = jnp.where(new_valid, qi[:, None] - key[None, :], -1)
    chunk_pos = (n_chunks - 1 - jnp.arange(n_chunks, dtype=jnp.int32))[:, None, None] * LANES + key[None, None, :]
    cache_dist = past + qi[None, :, None] - chunk_pos
    bias_odd = jnp.concatenate([_log_multiplicity(new_dist)[None], _log_multiplicity(cache_dist)], axis=0)
    bias_odd = jnp.repeat(bias_odd, 16, axis=1)

    cke = cache_k_even.reshape(cache_k_even.shape[:3] + (d,))
    cve = cache_v_even.reshape(cache_v_even.shape[:3] + (d,))
    cko = cache_k_odd.reshape(cache_k_odd.shape[:3] + (d,))
    cvo = cache_v_odd.reshape(cache_v_odd.shape[:3] + (d,))
    cmk = cache_mem_k.reshape(cache_mem_k.shape[:3] + (d,))
    cmv = cache_mem_v.reshape(cache_mem_v.shape[:3] + (d,))
    mem_rows = mem_prompt.reshape(b * n_mem, d)
    mem_tm = min(tm, b * n_mem)

    half = d // 2
    even_cols = jnp.concatenate([jnp.arange(half) + o * half for o in (0, 3, 1, 4, 2, 5)])

    def pad_new(x):
        x3 = x[rp:rp + rs].reshape(s, ts, d)
        return jnp.concatenate([x3, jnp.zeros((s, LANES - ts, d), x.dtype)], axis=1)

    outs = {n: [] for n in ("kep", "vep", "kes", "ves", "kop", "vop", "kos", "vos", "mkp", "mvp")}
    keep = min(MAX_WINDOW, t)

    for l in range(depth):
        even = l % 2 == 0
        w_l = w_in[l][:, even_cols] if even else w_in[l]
        q, kf, vf, kb, vb = _proj(h, norm_mix[l][None], w_l.astype(BF16), rope_tabs,
                                  half if even else d, tm)
        k_heads = kf[:rp].reshape(b, t, n_heads, HEAD_DIM)
        v_heads = vf[:rp].reshape(b, t, n_heads, HEAD_DIM)
        ks_heads = kf[rp:rp + rs].reshape(s, ts, n_heads, HEAD_DIM)
        vs_heads = vf[rp:rp + rs].reshape(s, ts, n_heads, HEAD_DIM)
        q_s = q[rp:rp + rs].reshape(s, ts, d).astype(F32)
        if even:
            e = l // 2
            lam_init = 0.8 - 0.6 * math.exp(-0.3 * l)
            lp = diff_lambda[e].astype(F32)
            lam = jnp.exp(jnp.sum(lp[0] * lp[1])) - jnp.exp(jnp.sum(lp[2] * lp[3])) + lam_init
            lam_vec = jnp.full((1, LANES), lam, F32)
            g_vec = diff_subln[e][None].astype(F32)
            n_pairs = half // LANES
            oa = _diff_attn(q, kb, vb, lam_vec, g_vec, b, t, n_pairs, 1.0 - lam_init, tq, tk)
            ob = _stick_attn(q, kb, vb, tri_p, b, t, n_pairs, n_pairs, tq, tk)
            attn_p = jnp.concatenate([oa, ob], axis=1)
            attn_s = _sample_attn(q_s, bias_even, pad_new(kb), pad_new(vb), cke, cve, e, even=True,
                                  page_table=page_table, lam_vec=lam_vec, g_vec=g_vec, tri=tri_s,
                                  out_scale=1.0 - lam_init)
            outs["kep"].append(k_heads); outs["vep"].append(v_heads)
            outs["kes"].append(ks_heads); outs["ves"].append(vs_heads)
        else:
            o_idx = l // 2
            attn_p = _window_attn(q, kb, vb, bias_window, b, t, d // LANES, tq)
            attn_s = _sample_attn(q_s, bias_odd, pad_new(kb), pad_new(vb), cko, cvo, o_idx, even=False)
            outs["kop"].append(k_heads[:, t - keep:]); outs["vop"].append(v_heads[:, t - keep:])
            outs["kos"].append(ks_heads); outs["vos"].append(vs_heads)
        attn = jnp.concatenate([attn_p, attn_s.reshape(rs, d).astype(BF16),
                                jnp.zeros((r_pad - rp - rs, d), BF16)], axis=0)
        h = _mm_res(attn, w_out[l].astype(BF16), h, tm, "attn_out_proj")

        mkv = _rms_mm(mem_rows, norm_mem_kv[l][None], w_kv_mem[l].astype(BF16), F32, mem_tm, "mem_kv_proj")
        mk, mv = mkv[:, :d], mkv[:, d:]
        outs["mkp"].append(mk.reshape(b, n_mem, MEM_HEADS, d // MEM_HEADS))
        outs["mvp"].append(mv.reshape(b, n_mem, MEM_HEADS, d // MEM_HEADS))
        qm = _rms_mm(h, norm_mem_q[l][None], w_q_mem[l].astype(BF16), BF16, tm, "mem_q_proj")
        om_p = _mem_prompt(qm, mk.astype(BF16), mv.astype(BF16), b, t, min(512, t))
        om_s = _mem_sample(qm[rp:rp + rs].reshape(s, ts, d).astype(F32), cmk, cmv, l, 2 if s % 2 == 0 else 1)
        om = jnp.concatenate([om_p, om_s.reshape(rs, d).astype(BF16),
                              jnp.zeros((r_pad - rp - rs, d), BF16)], axis=0)
        h = _mm_res(om, w_o_mem[l].astype(BF16), h, tm, "mem_out_proj")

        wgu = w_gate_up[l].astype(BF16)
        h = _ffn(h, norm_ffn[l][None], wgu[:, :d_ff], wgu[:, d_ff:], w_down[l].astype(BF16), tm)

    y = _rms(h, norm_final[None], tm)
    st = lambda n: jnp.stack(outs[n])
    return (y[:rp].reshape(b, t, d), y[rp:rp + rs].reshape(s, ts, d),
            st("kep"), st("vep"), st("kes"), st("ves"),
            st("kop"), st("vop"), st("kos"), st("vos"), st("mkp"), st("mvp"))
```

```python
import functools
import math

import jax
import jax.numpy as jnp
from jax import lax
from jax.experimental import pallas as pl
from jax.experimental.pallas import tpu as pltpu

F32 = jnp.float32
BF16 = jnp.bfloat16

HEAD_DIM = 64
ROT_DIM = HEAD_DIM // 4
ROPE_THETA = 500000.0
DILATED_PATTERNS = ((128, 1), (512, 4), (2048, 16))
MAX_WINDOW = 2048
MEM_HEADS = 4
RMS_EPS = 1e-6
NEG_INF = -1e30

LANES = 128
VMEM_LIMIT = 56 * 1024 * 1024

_NT = (((1,), (1,)), ((), ()))


def _cparams(sem):
    return pltpu.CompilerParams(dimension_semantics=sem, vmem_limit_bytes=VMEM_LIMIT)


def _dot(a, b):
    return jnp.dot(a, b, preferred_element_type=F32)


def _dot_nt(a, b):
    return lax.dot_general(a, b, _NT, preferred_element_type=F32)


def _rms_rows(x, g):
    return x * lax.rsqrt(jnp.mean(x * x, axis=-1, keepdims=True) + RMS_EPS) * g


def _resident(shape):
    nd = len(shape)
    return pl.BlockSpec(shape, lambda *_: (0,) * nd, pipeline_mode=pl.Buffered(1))


def _proj_kernel(x_ref, g_ref, w_ref, c_ref, s1_ref, s2_ref,
                 q_ref, kf_ref, vf_ref, kb_ref, vb_ref, *, d, rope_cols):
    hn = _rms_rows(x_ref[...], g_ref[...]).astype(BF16)
    cos, s_lo, s_hi = c_ref[...], s1_ref[...], s2_ref[...]

    def rope(xc):
        return xc * cos + pltpu.roll(xc, 8, 1) * s_lo + pltpu.roll(xc, LANES - 8, 1) * s_hi

    q = _dot(hn, w_ref[:, 0:d])
    k = _dot(hn, w_ref[:, d:2 * d])
    v = _dot(hn, w_ref[:, 2 * d:3 * d])
    for c in range(d // LANES):
        sl = slice(c * LANES, (c + 1) * LANES)
        qc, kc = q[:, sl], k[:, sl]
        if c * LANES < rope_cols:
            qc, kc = rope(qc), rope(kc)
        q_ref[:, sl] = (qc * (HEAD_DIM ** -0.5)).astype(BF16)
        kf_ref[:, sl] = kc
        kb_ref[:, sl] = kc.astype(BF16)
    vf_ref[...] = v
    vb_ref[...] = v.astype(BF16)


def _proj(h, g, w_bf, rope_tabs, rope_cols, tm):
    r, d = h.shape
    row = lambda i: (i, 0)
    tab = pl.BlockSpec((tm, LANES), row)
    blk = pl.BlockSpec((tm, d), row)
    return pl.pallas_call(
        functools.partial(_proj_kernel, d=d, rope_cols=rope_cols),
        grid=(r // tm,),
        in_specs=[blk, _resident((1, d)), _resident((d, 3 * d)), tab, tab, tab],
        out_specs=[blk] * 5,
        out_shape=[jax.ShapeDtypeStruct((r, d), t) for t in (BF16, F32, F32, BF16, BF16)],
        compiler_params=_cparams(("parallel",)),
        name="qkv_proj",
    )(h, g, w_bf, *rope_tabs)


def _rms_mm_kernel(x_ref, g_ref, w_ref, o_ref):
    hn = _rms_rows(x_ref[...], g_ref[...]).astype(BF16)
    o_ref[...] = _dot(hn, w_ref[...]).astype(o_ref.dtype)


def _rms_mm(h, g, w_bf, out_dtype, tm, name):
    r, d = h.shape
    n = w_bf.shape[1]
    return pl.pallas_call(
        _rms_mm_kernel,
        grid=(r // tm,),
        in_specs=[pl.BlockSpec((tm, d), lambda i: (i, 0)), _resident((1, d)), _resident((d, n))],
        out_specs=pl.BlockSpec((tm, n), lambda i: (i, 0)),
        out_shape=jax.ShapeDtypeStruct((r, n), out_dtype),
        compiler_params=_cparams(("parallel",)),
        name=name,
    )(h, g, w_bf)


def _mm_res_kernel(a_ref, w_ref, r_ref, o_ref):
    o_ref[...] = r_ref[...] + _dot(a_ref[...], w_ref[...])


def _mm_res(a_bf, w_bf, res, tm, name):
    r, k = a_bf.shape
    n = w_bf.shape[1]
    return pl.pallas_call(
        _mm_res_kernel,
        grid=(r // tm,),
        in_specs=[pl.BlockSpec((tm, k), lambda i: (i, 0)), _resident((k, n)),
                  pl.BlockSpec((tm, n), lambda i: (i, 0))],
        out_specs=pl.BlockSpec((tm, n), lambda i: (i, 0)),
        out_shape=jax.ShapeDtypeStruct((r, n), F32),
        compiler_params=_cparams(("parallel",)),
        name=name,
    )(a_bf, w_bf, res)


def _ffn_kernel(x_ref, g_ref, wg_ref, wu_ref, wd_ref, o_ref, *, f_chunk):
    x = x_ref[...]
    hn = _rms_rows(x, g_ref[...]).astype(BF16)
    acc = x
    for c in range(wg_ref.shape[1] // f_chunk):
        sl = slice(c * f_chunk, (c + 1) * f_chunk)
        gate = _dot(hn, wg_ref[:, sl])
        up = _dot(hn, wu_ref[:, sl])
        act = (gate / (1.0 + jnp.exp(-gate)) * up).astype(BF16)
        acc = acc + _dot(act, wd_ref[sl, :])
    o_ref[...] = acc


def _ffn(h, g, wg_bf, wu_bf, wd_bf, tm):
    r, d = h.shape
    f = wg_bf.shape[1]
    f_chunk = 256 if f % 256 == 0 else f
    return pl.pallas_call(
        functools.partial(_ffn_kernel, f_chunk=f_chunk),
        grid=(r // tm,),
        in_specs=[pl.BlockSpec((tm, d), lambda i: (i, 0)), _resident((1, d)),
                  _resident((d, f)), _resident((d, f)), _resident((f, d))],
        out_specs=pl.BlockSpec((tm, d), lambda i: (i, 0)),
        out_shape=jax.ShapeDtypeStruct((r, d), F32),
        compiler_params=_cparams(("parallel",)),
        name="swiglu",
    )(h, g, wg_bf, wu_bf, wd_bf)


def _rms_kernel(x_ref, g_ref, o_ref):
    o_ref[...] = _rms_rows(x_ref[...], g_ref[...])


def _rms(h, g, tm):
    r, d = h.shape
    return pl.pallas_call(
        _rms_kernel,
        grid=(r // tm,),
        in_specs=[pl.BlockSpec((tm, d), lambda i: (i, 0)), _resident((1, d))],
        out_specs=pl.BlockSpec((tm, d), lambda i: (i, 0)),
        out_shape=jax.ShapeDtypeStruct((r, d), F32),
        compiler_params=_cparams(("parallel",)),
        name="final_rms",
    )(h, g)


def _stack_slots(q):
    lane = lax.broadcasted_iota(jnp.int32, (1, LANES), 1)
    zero = jnp.zeros_like(q)
    return jnp.concatenate([jnp.where(lane < HEAD_DIM, q, zero),
                            jnp.where(lane >= HEAD_DIM, q, zero)], axis=0)


def _merge_slots(acc, tq):
    lane = lax.broadcasted_iota(jnp.int32, (1, LANES), 1)
    return jnp.where(lane < HEAD_DIM, acc[:tq], acc[tq:])


def _row_groups(tq, rg):
    return [(r0, slice(sl * tq + r0, sl * tq + r0 + rg)) for sl in range(2) for r0 in range(0, tq, rg)]


def _kv_block(ref, j, tk):
    return ref[pl.ds(pl.multiple_of(j * tk, tk), tk), :]


def _lane_tiles(x):
    return [x[:, t * LANES:(t + 1) * LANES] for t in range(x.shape[1] // LANES)]


def _with_ones(vb):
    return jnp.concatenate([vb, jnp.ones(vb.shape, vb.dtype)], axis=1)


def _softmax_update(s, vb1, rows, m_sc, acc_sc, first):
    tiles = _lane_tiles(s)
    m_cur = jnp.max(functools.reduce(jnp.maximum, tiles), axis=-1, keepdims=True)
    if first:
        m_new = jnp.broadcast_to(m_cur, tiles[0].shape)
    else:
        m_old = m_sc[rows, :]
        m_new = jnp.maximum(m_old, m_cur)
    p = jnp.concatenate([jnp.exp(t - m_new) for t in tiles], axis=1).astype(BF16)
    pv = _dot(p, vb1)
    if first:
        acc_sc[rows, :] = pv
    else:
        alpha = jnp.exp(m_old - m_new)
        acc_sc[rows, :] = jnp.concatenate([alpha, alpha], axis=1) * acc_sc[rows, :] + pv
    m_sc[rows, :] = m_new


def _diff_kernel(lam_ref, g_ref, q_ref, k_ref, v_ref, o_ref, m_sc, acc_sc, *, tq, tk, rg, out_scale):
    i = pl.program_id(2)
    qq = _stack_slots(q_ref[...])
    jd = (i * tq) // tk
    groups = _row_groups(tq, rg)

    def sweep(j, first):
        kb, vb1 = _kv_block(k_ref, j, tk), _with_ones(_kv_block(v_ref, j, tk))
        for r0, rows in groups:
            s = _dot_nt(qq[rows], kb)
            if first:
                qpos = i * tq + r0 + lax.broadcasted_iota(jnp.int32, (rg, tk), 0)
                kpos = jd * tk + lax.broadcasted_iota(jnp.int32, (rg, tk), 1)
                s = jnp.where(kpos <= qpos, s, NEG_INF)
            _softmax_update(s, vb1, rows, m_sc, acc_sc, first)

    sweep(jd, True)

    def body(j, carry):
        sweep(j, False)
        return carry

    lax.fori_loop(0, jd, body, 0)

    a = acc_sc[:, 0:LANES] / acc_sc[:, LANES:2 * LANES]
    o = a[:tq] - lam_ref[...] * a[tq:]
    o_ref[...] = (_rms_rows(o, g_ref[...]) * out_scale).astype(o_ref.dtype)


def _softplus(z):
    return jnp.maximum(z, 0.0) + jnp.log(1.0 + jnp.exp(-jnp.abs(z)))


def _stick_weights_tiled(z, tri, carry):
    sub = tri.shape[0]
    lstay = -_softplus(z)
    hi = lstay.astype(BF16)
    lo = (lstay - hi.astype(F32)).astype(BF16)
    out = [None] * (z.shape[1] // sub)
    for h in reversed(range(len(out))):
        cols = slice(h * sub, (h + 1) * sub)
        c_tiles = _lane_tiles(_dot(hi[:, cols], tri) + _dot(lo[:, cols], tri))
        if carry is not None:
            c_tiles = [c + carry for c in c_tiles]
        out[h] = jnp.exp(z[:, cols] + jnp.concatenate(c_tiles, axis=1))
        carry = jnp.broadcast_to(c_tiles[0][:, 0:1], c_tiles[0].shape)
    return jnp.concatenate(out, axis=1), carry


def _stick_kernel(tri_ref, q_ref, k_ref, v_ref, o_ref, c_sc, acc_sc, *, tq, tk, rg):
    i = pl.program_id(2)
    qq = _stack_slots(q_ref[...])
    tri = tri_ref[...]
    jd = (i * tq) // tk
    groups = _row_groups(tq, rg)

    def sweep(j, first):
        kb, vb = _kv_block(k_ref, j, tk), _kv_block(v_ref, j, tk)
        for r0, rows in groups:
            z = _dot_nt(qq[rows], kb)
            if first:
                qpos = i * tq + r0 + lax.broadcasted_iota(jnp.int32, (rg, tk), 0)
                kpos = jd * tk + lax.broadcasted_iota(jnp.int32, (rg, tk), 1)
                z = jnp.where(kpos < qpos, z, NEG_INF)
            w, c = _stick_weights_tiled(z, tri, None if first else c_sc[rows, :])
            c_sc[rows, :] = c
            pv = _dot(w.astype(BF16), vb)
            acc_sc[rows, :] = pv if first else acc_sc[rows, :] + pv

    sweep(jd, True)

    def body(t, carry):
        sweep(jd - 1 - t, False)
        return carry

    lax.fori_loop(0, jd, body, 0)
    o_ref[...] = _merge_slots(acc_sc[...], tq).astype(o_ref.dtype)


def _window_kernel(bias_ref, q_ref, k_ref, v_ref, o_ref, m_sc, acc_sc, *, tq, tk, rg):
    i = pl.program_id(2)
    qq = _stack_slots(q_ref[...])
    groups = _row_groups(tq, rg)
    ratio = tk // tq
    jd = (i * tq) // tk
    j_lo = jnp.maximum(i * tq - MAX_WINDOW, 0) // tk

    def sweep(j, first):
        kb, vb1 = _kv_block(k_ref, j, tk), _with_ones(_kv_block(v_ref, j, tk))
        for r0, rows in groups:
            s = _dot_nt(qq[rows], kb) + bias_ref[i - j * ratio, r0:r0 + rg, :]
            _softmax_update(s, vb1, rows, m_sc, acc_sc, first)

    sweep(jd, True)

    def body(t, carry):
        sweep(jd - 1 - t, False)
        return carry

    lax.fori_loop(0, jd - j_lo, body, 0)
    o_ref[...] = _merge_slots(acc_sc[:, 0:LANES] / acc_sc[:, LANES:2 * LANES], tq).astype(o_ref.dtype)


def _prompt_attn_specs(b, t, tq, col0):
    nq = t // tq
    qspec = pl.BlockSpec((tq, LANES), lambda bi, p, i: (bi * nq + i, col0 + p))
    kvspec = pl.BlockSpec((t, LANES), lambda bi, p, i: (bi, col0 + p))
    ospec = pl.BlockSpec((tq, LANES), lambda bi, p, i: (bi * nq + i, p))
    return nq, qspec, kvspec, ospec


def _diff_attn(q, k, v, lam_vec, g_vec, b, t, n_pairs, out_scale, tq, tk, rg):
    nq, qspec, kvspec, ospec = _prompt_attn_specs(b, t, tq, 0)
    return pl.pallas_call(
        functools.partial(_diff_kernel, tq=tq, tk=tk, rg=rg, out_scale=out_scale),
        grid=(b, n_pairs, nq),
        in_specs=[_resident((1, LANES)), _resident((1, LANES)), qspec, kvspec, kvspec],
        out_specs=ospec,
        out_shape=jax.ShapeDtypeStruct((b * t, n_pairs * LANES), BF16),
        scratch_shapes=[pltpu.VMEM((2 * tq, LANES), F32), pltpu.VMEM((2 * tq, 2 * LANES), F32)],
        compiler_params=_cparams(("parallel", "parallel", "arbitrary")),
        name=f"diff_attn_g{rg}",
    )(lam_vec, g_vec, q, k, v)


def _stick_attn(q, k, v, tri, b, t, n_pairs, col0, tq, tk, rg):
    nq, qspec, kvspec, ospec = _prompt_attn_specs(b, t, tq, col0)
    return pl.pallas_call(
        functools.partial(_stick_kernel, tq=tq, tk=tk, rg=rg),
        grid=(b, n_pairs, nq),
        in_specs=[_resident(tri.shape), qspec, kvspec, kvspec],
        out_specs=ospec,
        out_shape=jax.ShapeDtypeStruct((b * t, n_pairs * LANES), BF16),
        scratch_shapes=[pltpu.VMEM((2 * tq, LANES), F32), pltpu.VMEM((2 * tq, LANES), F32)],
        compiler_params=_cparams(("parallel", "parallel", "arbitrary")),
        name=f"stick_attn_g{rg}",
    )(tri, q, k, v)


def _window_attn(q, k, v, bias, b, t, n_pairs, tq, rg):
    nq, qspec, kvspec, ospec = _prompt_attn_specs(b, t, tq, 0)
    return pl.pallas_call(
        functools.partial(_window_kernel, tq=tq, tk=bias.shape[2], rg=rg),
        grid=(b, n_pairs, nq),
        in_specs=[_resident(bias.shape), qspec, kvspec, kvspec],
        out_specs=ospec,
        out_shape=jax.ShapeDtypeStruct((b * t, n_pairs * LANES), BF16),
        scratch_shapes=[pltpu.VMEM((2 * tq, LANES), F32), pltpu.VMEM((2 * tq, 2 * LANES), F32)],
        compiler_params=_cparams(("parallel", "parallel", "arbitrary")),
        name=f"window_attn_g{rg}",
    )(bias, q, k, v)


def _log_multiplicity(dist):
    cnt = jnp.zeros(dist.shape, F32)
    for window, dil in DILATED_PATTERNS:
        cnt = cnt + ((dist >= 0) & (dist <= window) & (dist % dil == 0)).astype(F32)
    return jnp.where(cnt > 0, jnp.log(jnp.maximum(cnt, 1.0)), NEG_INF)


def _slot_mask(n_rows, d, first_slot, width):
    r = lax.broadcasted_iota(jnp.int32, (n_rows, d), 0) + first_slot
    c = lax.broadcasted_iota(jnp.int32, (n_rows, d), 1)
    return jnp.right_shift(c, width.bit_length() - 1) == r


def _sample_kernel(*refs, even, ts, d, pps, out_scale):
    if even:
        _, lam_ref, g_ref, tri_ref = refs[:4]
        refs = refs[4:]
        n_soft = ts * 8
    else:
        n_soft = ts * 16
    q_ref, bn_ref, bc_ref, kn_ref, vn_ref = refs[:5]
    kc_refs, vc_refs = refs[5:5 + pps], refs[5 + pps:5 + 2 * pps]
    o_ref, qbd_sc, m_sc, l_sc, acc_sc = refs[5 + 2 * pps:10 + 2 * pps]
    c_sc = refs[10 + 2 * pps] if even else None
    c = pl.program_id(1)
    n_rows = ts * 16

    def process(kc, vc, bias, first):
        s = _dot_nt(qbd_sc[...], kc) + bias
        ss = s[:n_soft]
        if first:
            m_new = jnp.max(ss, axis=-1, keepdims=True)
            p = jnp.exp(ss - m_new)
            l_sc[...] = jnp.sum(p, axis=-1, keepdims=True)
        else:
            m_old = m_sc[...]
            m_new = jnp.maximum(m_old, jnp.max(ss, axis=-1, keepdims=True))
            alpha = jnp.exp(m_old - m_new)
            p = jnp.exp(ss - m_new)
            l_sc[...] = alpha * l_sc[...] + jnp.sum(p, axis=-1, keepdims=True)
        m_sc[...] = m_new
        if even:
            w, cnew = _stick_weights_tiled(s[n_soft:], tri_ref[...], None if first else c_sc[...])
            c_sc[...] = cnew
            p = jnp.concatenate([p, w], axis=0)
        pv = _dot(p.astype(BF16), vc)
        if first:
            acc_sc[...] = pv
        else:
            acc_sc[0:n_soft] = alpha * acc_sc[0:n_soft] + pv[:n_soft]
            if even:
                acc_sc[n_soft:n_rows] = acc_sc[n_soft:n_rows] + pv[n_soft:]

    @pl.when(c == 0)
    def _():
        q = q_ref[0]
        if even:
            md = _slot_mask(8, d, 0, HEAD_DIM)
            mb = _slot_mask(8, d, 8, HEAD_DIM)
            rows = [jnp.where(md, jnp.broadcast_to(q[t:t + 1], (8, d)), 0.0) for t in range(ts)]
            rows += [jnp.where(mb, jnp.broadcast_to(q[t:t + 1], (8, d)), 0.0) for t in range(ts)]
        else:
            ma = _slot_mask(16, d, 0, HEAD_DIM)
            rows = [jnp.where(ma, jnp.broadcast_to(q[t:t + 1], (16, d)), 0.0) for t in range(ts)]
        qbd_sc[...] = jnp.concatenate(rows, axis=0).astype(BF16)
        process(kn_ref[0], vn_ref[0], bn_ref[...], True)

    @pl.when(c > 0)
    def _():
        kc = jnp.concatenate([r[...].astype(BF16) for r in kc_refs], axis=0)
        vc = jnp.concatenate([r[...].astype(BF16) for r in vc_refs], axis=0)
        process(kc, vc, bc_ref[0], False)

    @pl.when(c == pl.num_programs(1) - 1)
    def _():
        acc = acc_sc[...]
        if even:
            a = (acc[:n_soft] / l_sc[...]).reshape(ts, 8, d)
            r = lax.broadcasted_iota(jnp.int32, (8, d), 0)
            col = lax.broadcasted_iota(jnp.int32, (8, d), 1)
            head2 = 2 * jnp.right_shift(col, (2 * HEAD_DIM).bit_length() - 1)
            pick0 = (r == head2)[None]
            pick1 = (r == head2 + 1)[None]
            o0 = jnp.sum(jnp.where(pick0, a, 0.0), axis=1)
            o1 = jnp.sum(jnp.where(pick1, a, 0.0), axis=1)
            ob = jnp.sum(jnp.where(_slot_mask(8, d, 8, HEAD_DIM)[None],
                                   acc[n_soft:].reshape(ts, 8, d), 0.0), axis=1)
            half = d // 2
            for hd in range(half // LANES):
                sl = slice(hd * LANES, (hd + 1) * LANES)
                oa = o0[:, sl] - lam_ref[...] * o1[:, sl]
                o_ref[0, :, sl] = (_rms_rows(oa, g_ref[...]) * out_scale).astype(o_ref.dtype)
            o_ref[0, :, half:] = ob[:, half:].astype(o_ref.dtype)
        else:
            a = (acc / l_sc[...]).reshape(ts, 16, d)
            o = jnp.sum(jnp.where(_slot_mask(16, d, 0, HEAD_DIM)[None], a, 0.0), axis=1)
            o_ref[0] = o.astype(o_ref.dtype)


def _sample_attn(q3, bias_new, bias_cache, k_new, v_new, cache_k, cache_v, layer, *, even, page_table=None,
                 lam_vec=None, g_vec=None, tri=None, out_scale=1.0):
    s, ts, d = q3.shape
    n_groups, n_rows, gk = bias_cache.shape
    pps = gk // LANES
    group = lambda c: n_groups - jnp.maximum(c, 1)
    if even:
        maps = lambda f: (lambda si, c, pt: f(si, c, pt))
        cache_maps = [lambda si, c, pt, p=p: (layer, pt[si, group(c) * pps + p], 0, 0) for p in range(pps)]
    else:
        maps = lambda f: (lambda si, c: f(si, c, None))
        cache_maps = [lambda si, c, p=p: (layer, si, group(c) * pps + p, 0) for p in range(pps)]
    seq3 = maps(lambda si, c, pt: (si, 0, 0))
    const2 = maps(lambda si, c, pt: (0, 0))
    grp3 = maps(lambda si, c, pt: (jnp.maximum(c, 1) - 1, 0, 0))
    cache_specs = [pl.BlockSpec((None, None, LANES, d), m) for m in cache_maps]
    in_specs = [pl.BlockSpec((1, ts, d), seq3), pl.BlockSpec((n_rows, LANES), const2),
                pl.BlockSpec((1, n_rows, gk), grp3),
                pl.BlockSpec((1, LANES, d), seq3), pl.BlockSpec((1, LANES, d), seq3)]
    in_specs += cache_specs + cache_specs
    args = [q3, bias_new, bias_cache, k_new, v_new] + [cache_k] * pps + [cache_v] * pps
    scratch = [pltpu.VMEM((n_rows, d), BF16),
               pltpu.VMEM((ts * 8 if even else n_rows, 1), F32),
               pltpu.VMEM((ts * 8 if even else n_rows, 1), F32),
               pltpu.VMEM((n_rows, d), F32)]
    if even:
        in_specs = [pl.BlockSpec((1, LANES), const2), pl.BlockSpec((1, LANES), const2),
                    pl.BlockSpec((LANES, LANES), const2)] + in_specs
        args = [page_table, lam_vec, g_vec, tri] + args
        scratch.append(pltpu.VMEM((ts * 8, LANES), F32))
    grid_spec = pltpu.PrefetchScalarGridSpec(
        num_scalar_prefetch=1 if even else 0,
        grid=(s, n_groups + 1),
        in_specs=in_specs,
        out_specs=pl.BlockSpec((1, ts, d), seq3),
        scratch_shapes=scratch,
    )
    return pl.pallas_call(
        functools.partial(_sample_kernel, even=even, ts=ts, d=d, pps=pps, out_scale=out_scale),
        grid_spec=grid_spec,
        out_shape=jax.ShapeDtypeStruct((s, ts, d), F32),
        compiler_params=_cparams(("parallel", "arbitrary")),
        name="sample_attn_even" if even else "sample_attn_odd",
    )(*args)


def _mem_heads(q, k, v, o_store):
    d = q.shape[-1]
    hd = d // MEM_HEADS
    for h in range(MEM_HEADS):
        sl = slice(h * hd, (h + 1) * hd)
        s = _dot_nt(q[:, sl].astype(BF16), k[:, sl]) * (hd ** -0.5)
        p = jnp.exp(s - jnp.max(s, axis=-1, keepdims=True))
        o = _dot(p.astype(BF16), v[:, sl]) / jnp.sum(p, axis=-1, keepdims=True)
        o_store(sl, o)


def _mem_prompt_kernel(q_ref, k_ref, v_ref, o_ref):
    def store(sl, o):
        o_ref[:, sl] = o.astype(o_ref.dtype)
    _mem_heads(q_ref[...], k_ref[...], v_ref[...], store)


def _mem_prompt(q, mk_bf, mv_bf, b, t, tq):
    d = q.shape[1]
    n_mem = mk_bf.shape[0] // b
    nq = t // tq
    return pl.pallas_call(
        _mem_prompt_kernel,
        grid=(b, nq),
        in_specs=[pl.BlockSpec((tq, d), lambda bi, i: (bi * nq + i, 0)),
                  pl.BlockSpec((n_mem, d), lambda bi, i: (bi, 0)),
                  pl.BlockSpec((n_mem, d), lambda bi, i: (bi, 0))],
        out_specs=pl.BlockSpec((tq, d), lambda bi, i: (bi * nq + i, 0)),
        out_shape=jax.ShapeDtypeStruct((b * t, d), BF16),
        compiler_params=_cparams(("parallel", "parallel")),
        name="mem_attn_prompt",
    )(q, mk_bf, mv_bf)


def _mem_sample_kernel(q_ref, k_ref, v_ref, o_ref, *, n_seq):
    for si in range(n_seq):
        def store(sl, o, si=si):
            o_ref[si, :, sl] = o.astype(o_ref.dtype)
        _mem_heads(q_ref[si], k_ref[si].astype(BF16), v_ref[si].astype(BF16), store)


def _mem_sample(q3, cache_k, cache_v, layer, n_seq):
    s, ts, d = q3.shape
    n_mem = cache_k.shape[2]
    kv = pl.BlockSpec((None, n_seq, n_mem, d), lambda i: (layer, i, 0, 0))
    return pl.pallas_call(
        functools.partial(_mem_sample_kernel, n_seq=n_seq),
        grid=(s // n_seq,),
        in_specs=[pl.BlockSpec((n_seq, ts, d), lambda i: (i, 0, 0)), kv, kv],
        out_specs=pl.BlockSpec((n_seq, ts, d), lambda i: (i, 0, 0)),
        out_shape=jax.ShapeDtypeStruct((s, ts, d), F32),
        compiler_params=_cparams(("parallel",)),
        name="mem_attn_sample",
    )(q3, cache_k, cache_v)


def _rope_tables(pos):
    half = ROT_DIM // 2
    inv = jnp.power(jnp.float32(ROPE_THETA), -jnp.arange(half, dtype=F32) / half)
    ang = pos.astype(F32)[:, None] * inv[None, :]
    cos, sin = jnp.cos(ang), jnp.sin(ang)
    n = pos.shape[0]
    pad = jnp.zeros((n, HEAD_DIM - ROT_DIM), F32)
    z = jnp.zeros((n, half), F32)
    c_head = jnp.concatenate([cos, cos, pad + 1.0], axis=1)
    lo_head = jnp.concatenate([z, sin, pad], axis=1)
    hi_head = jnp.concatenate([-sin, z, pad], axis=1)
    rep = LANES // HEAD_DIM
    return tuple(jnp.tile(a, (1, rep)) for a in (c_head, lo_head, hi_head))


def kernel(x_prompt, x_sample, cache_k_even, cache_v_even, cache_k_odd, cache_v_odd, cache_mem_k, cache_mem_v, page_table, mem_prompt, w_in, w_out, diff_lambda, diff_subln, norm_mix, norm_mem_q, norm_mem_kv, w_q_mem, w_kv_mem, w_o_mem, norm_ffn, w_gate_up, w_down, norm_final):
    b, t, d = x_prompt.shape
    s, ts, _ = x_sample.shape
    depth = w_in.shape[0]
    n_heads = d // HEAD_DIM
    n_pages, page = page_table.shape[1], cache_k_even.shape[2]
    past = n_pages * page
    cwin = cache_k_odd.shape[2]
    n_mem = mem_prompt.shape[1]
    d_ff = w_down.shape[1]
    assert d % (2 * LANES) == 0 and page == LANES and cwin % LANES == 0 and past == cwin

    rp, rs = b * t, s * ts
    tm = 512
    r_pad = -(-(rp + rs) // tm) * tm
    tq = min(256, t)
    tk = min(512, t)
    assert t % tq == 0 and t % tk == 0 and tk % tq == 0 and rp % tm == 0

    h = jnp.concatenate([x_prompt.reshape(rp, d), x_sample.reshape(rs, d),
                         jnp.zeros((r_pad - rp - rs, d), F32)], axis=0)
    pos = jnp.concatenate([jnp.tile(jnp.arange(t, dtype=jnp.int32), b),
                           jnp.tile(past + jnp.arange(ts, dtype=jnp.int32), s),
                           jnp.zeros((r_pad - rp - rs,), jnp.int32)])
    rope_tabs = _rope_tables(pos)

    sub = min(2 * LANES, tk)
    tri_p = jnp.tril(jnp.ones((sub, sub), F32)).astype(BF16)
    tri_s = jnp.tril(jnp.ones((LANES, LANES), F32)).astype(BF16)
    n_tab = (MAX_WINDOW + tk - 1) // tq + 1
    dist = (jnp.arange(n_tab, dtype=jnp.int32)[:, None, None] * tq
            + jnp.arange(tq, dtype=jnp.int32)[None, :, None] - jnp.arange(tk, dtype=jnp.int32)[None, None, :])
    bias_window = _log_multiplicity(dist)

    n_chunks = past // LANES
    qi = jnp.arange(ts, dtype=jnp.int32)
    key = jnp.arange(LANES, dtype=jnp.int32)
    new_valid = key[None, :] < ts
    causal_new = jnp.where(new_valid & (key[None, :] <= qi[:, None]), 0.0, NEG_INF)
    strict_new = jnp.where(new_valid & (key[None, :] < qi[:, None]), 0.0, NEG_INF)
    rows8 = lambda a: jnp.repeat(a, 8, axis=0)
    pps_even = max(1, min(8, n_chunks // 2))
    assert n_chunks % pps_even == 0
    bias_even_new = jnp.concatenate([rows8(causal_new), rows8(strict_new)], axis=0).astype(F32)
    bias_even_cache = jnp.zeros((n_chunks // pps_even, ts * 16, pps_even * LANES), F32)
    new_dist = jnp.where(new_valid, qi[:, None] - key[None, :], -1)
    big_dil = DILATED_PATTERNS[-1][1]
    near_len = min(DILATED_PATTERNS[-2][0], cwin)
    far_len = cwin - near_len
    assert cwin % big_dil == 0 and far_len % big_dil == 0 and ts <= big_dil
    far_pos = (jnp.arange(far_len // big_dil, dtype=jnp.int32)[:, None] * big_dil
               + jnp.arange(ts, dtype=jnp.int32)[None, :]).reshape(-1)
    sel_pos = (past - cwin) + jnp.concatenate([far_pos, far_len + jnp.arange(near_len, dtype=jnp.int32)])
    n_sel = sel_pos.shape[0]
    assert n_sel % LANES == 0
    sel_dist = past + qi[:, None] - sel_pos[None, :]
    sel_chunks = n_sel // LANES
    pps_odd = max(p for p in range(1, 9) if sel_chunks % p == 0)
    bias_odd_cache = _log_multiplicity(sel_dist).reshape(ts, sel_chunks // pps_odd, pps_odd * LANES)
    bias_odd_cache = jnp.repeat(bias_odd_cache.transpose(1, 0, 2)[::-1], 16, axis=1)
    bias_odd_new = jnp.repeat(_log_multiplicity(new_dist), 16, axis=0)

    def select_rows(cache):
        n = cache.shape[0]
        far = cache[:, :, :far_len].reshape(n, s, far_len // big_dil, big_dil, d)[:, :, :, :ts]
        return jnp.concatenate([far.reshape(n, s, -1, d), cache[:, :, far_len:].reshape(n, s, near_len, d)], axis=2)

    cke = cache_k_even.reshape(cache_k_even.shape[:3] + (d,))
    cve = cache_v_even.reshape(cache_v_even.shape[:3] + (d,))
    cko = select_rows(cache_k_odd)
    cvo = select_rows(cache_v_odd)
    cmk = cache_mem_k.reshape(cache_mem_k.shape[:3] + (d,))
    cmv = cache_mem_v.reshape(cache_mem_v.shape[:3] + (d,))
    mem_rows = mem_prompt.reshape(b * n_mem, d)
    mem_tm = min(tm, b * n_mem)

    half = d // 2
    even_cols = jnp.concatenate([jnp.arange(half) + o * half for o in (0, 3, 1, 4, 2, 5)])

    def pad_new(x):
        x3 = x[rp:rp + rs].reshape(s, ts, d)
        return jnp.concatenate([x3, jnp.zeros((s, LANES - ts, d), x.dtype)], axis=1)

    outs = {n: [] for n in ("kep", "vep", "kes", "ves", "kop", "vop", "kos", "vos", "mkp", "mvp")}
    keep = min(MAX_WINDOW, t)

    for l in range(depth):
        even = l % 2 == 0
        w_l = w_in[l][:, even_cols] if even else w_in[l]
        q, kf, vf, kb, vb = _proj(h, norm_mix[l][None], w_l.astype(BF16), rope_tabs,
                                  half if even else d, tm)
        k_heads = kf[:rp].reshape(b, t, n_heads, HEAD_DIM)
        v_heads = vf[:rp].reshape(b, t, n_heads, HEAD_DIM)
        ks_heads = kf[rp:rp + rs].reshape(s, ts, n_heads, HEAD_DIM)
        vs_heads = vf[rp:rp + rs].reshape(s, ts, n_heads, HEAD_DIM)
        q_s = q[rp:rp + rs].reshape(s, ts, d).astype(F32)
        if even:
            e = l // 2
            lam_init = 0.8 - 0.6 * math.exp(-0.3 * l)
            lp = diff_lambda[e].astype(F32)
            lam = jnp.exp(jnp.sum(lp[0] * lp[1])) - jnp.exp(jnp.sum(lp[2] * lp[3])) + lam_init
            lam_vec = jnp.full((1, LANES), lam, F32)
            g_vec = diff_subln[e][None].astype(F32)
            n_pairs = half // LANES
            oa = _diff_attn(q, kb, vb, lam_vec, g_vec, b, t, n_pairs, 1.0 - lam_init, tq, tk, min(tq, 128))
            ob = _stick_attn(q, kb, vb, tri_p, b, t, n_pairs, n_pairs, tq, tk, tq)
            attn_p = jnp.concatenate([oa, ob], axis=1)
            attn_s = _sample_attn(q_s, bias_even_new, bias_even_cache, pad_new(kb), pad_new(vb), cke, cve, e, even=True,
                                  page_table=page_table, lam_vec=lam_vec, g_vec=g_vec, tri=tri_s,
                                  out_scale=1.0 - lam_init)
            outs["kep"].append(k_heads); outs["vep"].append(v_heads)
            outs["kes"].append(ks_heads); outs["ves"].append(vs_heads)
        else:
            o_idx = l // 2
            attn_p = _window_attn(q, kb, vb, bias_window, b, t, d // LANES, tq, min(tq, 128))
            attn_s = _sample_attn(q_s, bias_odd_new, bias_odd_cache, pad_new(kb), pad_new(vb), cko, cvo, o_idx,
                                  even=False)
            outs["kop"].append(k_heads[:, t - keep:]); outs["vop"].append(v_heads[:, t - keep:])
            outs["kos"].append(ks_heads); outs["vos"].append(vs_heads)
        attn = jnp.concatenate([attn_p, attn_s.reshape(rs, d).astype(BF16),
                                jnp.zeros((r_pad - rp - rs, d), BF16)], axis=0)
        h = _mm_res(attn, w_out[l].astype(BF16), h, tm, "attn_out_proj")

        mkv = _rms_mm(mem_rows, norm_mem_kv[l][None], w_kv_mem[l].astype(BF16), F32, mem_tm, "mem_kv_proj")
        mk, mv = mkv[:, :d], mkv[:, d:]
        outs["mkp"].append(mk.reshape(b, n_mem, MEM_HEADS, d // MEM_HEADS))
        outs["mvp"].append(mv.reshape(b, n_mem, MEM_HEADS, d // MEM_HEADS))
        qm = _rms_mm(h, norm_mem_q[l][None], w_q_mem[l].astype(BF16), BF16, tm, "mem_q_proj")
        om_p = _mem_prompt(qm, mk.astype(BF16), mv.astype(BF16), b, t, min(512, t))
        om_s = _mem_sample(qm[rp:rp + rs].reshape(s, ts, d).astype(F32), cmk, cmv, l, 2 if s % 2 == 0 else 1)
        om = jnp.concatenate([om_p, om_s.reshape(rs, d).astype(BF16),
                              jnp.zeros((r_pad - rp - rs, d), BF16)], axis=0)
        h = _mm_res(om, w_o_mem[l].astype(BF16), h, tm, "mem_out_proj")

        wgu = w_gate_up[l].astype(BF16)
        h = _ffn(h, norm_ffn[l][None], wgu[:, :d_ff], wgu[:, d_ff:], w_down[l].astype(BF16), tm)

    y = _rms(h, norm_final[None], tm)
    st = lambda n: jnp.stack(outs[n])
    return (y[:rp].reshape(b, t, d), y[rp:rp + rs].reshape(s, ts, d),
            st("kep"), st("vep"), st("kes"), st("ves"),
            st("kop"), st("vop"), st("kos"), st("vos"), st("mkp"), st("mvp"))
```

```python
import functools
import math

import jax
import jax.numpy as jnp
from jax import lax
from jax.experimental import pallas as pl
from jax.experimental.pallas import tpu as pltpu

F32 = jnp.float32
BF16 = jnp.bfloat16

HEAD_DIM = 64
ROT_DIM = HEAD_DIM // 4
ROPE_THETA = 500000.0
DILATED_PATTERNS = ((128, 1), (512, 4), (2048, 16))
MAX_WINDOW = 2048
MEM_HEADS = 4
RMS_EPS = 1e-6
NEG_INF = -1e30
STICK_DEAD = -110.0

LANES = 128
VMEM_LIMIT = 56 * 1024 * 1024

_NT = (((1,), (1,)), ((), ()))


def _cparams(sem):
    return pltpu.CompilerParams(dimension_semantics=sem, vmem_limit_bytes=VMEM_LIMIT)


def _dot(a, b):
    return jnp.dot(a, b, preferred_element_type=F32)


def _dot_nt(a, b):
    return lax.dot_general(a, b, _NT, preferred_element_type=F32)


def _rms_rows(x, g):
    return x * lax.rsqrt(jnp.mean(x * x, axis=-1, keepdims=True) + RMS_EPS) * g


def _resident(shape):
    nd = len(shape)
    return pl.BlockSpec(shape, lambda *_: (0,) * nd, pipeline_mode=pl.Buffered(1))


def _proj_kernel(x_ref, g_ref, w_ref, c_ref, s1_ref, s2_ref,
                 q_ref, kf_ref, vf_ref, kb_ref, vb_ref, *, d, rope_cols):
    hn = _rms_rows(x_ref[...], g_ref[...]).astype(BF16)
    cos, s_lo, s_hi = c_ref[...], s1_ref[...], s2_ref[...]

    def rope(xc):
        return xc * cos + pltpu.roll(xc, 8, 1) * s_lo + pltpu.roll(xc, LANES - 8, 1) * s_hi

    q = _dot(hn, w_ref[:, 0:d])
    k = _dot(hn, w_ref[:, d:2 * d])
    v = _dot(hn, w_ref[:, 2 * d:3 * d])
    for c in range(d // LANES):
        sl = slice(c * LANES, (c + 1) * LANES)
        qc, kc = q[:, sl], k[:, sl]
        if c * LANES < rope_cols:
            qc, kc = rope(qc), rope(kc)
        q_ref[:, sl] = (qc * (HEAD_DIM ** -0.5)).astype(BF16)
        kf_ref[:, sl] = kc
        kb_ref[:, sl] = kc.astype(BF16)
    vf_ref[...] = v
    vb_ref[...] = v.astype(BF16)


def _proj(h, g, w_bf, rope_tabs, rope_cols, tm):
    r, d = h.shape
    row = lambda i: (i, 0)
    tab = pl.BlockSpec((tm, LANES), row)
    blk = pl.BlockSpec((tm, d), row)
    return pl.pallas_call(
        functools.partial(_proj_kernel, d=d, rope_cols=rope_cols),
        grid=(r // tm,),
        in_specs=[blk, _resident((1, d)), _resident((d, 3 * d)), tab, tab, tab],
        out_specs=[blk] * 5,
        out_shape=[jax.ShapeDtypeStruct((r, d), t) for t in (BF16, F32, F32, BF16, BF16)],
        compiler_params=_cparams(("parallel",)),
        name="qkv_proj",
    )(h, g, w_bf, *rope_tabs)


def _rms_mm_kernel(x_ref, g_ref, w_ref, o_ref):
    hn = _rms_rows(x_ref[...], g_ref[...]).astype(BF16)
    o_ref[...] = _dot(hn, w_ref[...]).astype(o_ref.dtype)


def _rms_mm(h, g, w_bf, out_dtype, tm, name):
    r, d = h.shape
    n = w_bf.shape[1]
    return pl.pallas_call(
        _rms_mm_kernel,
        grid=(r // tm,),
        in_specs=[pl.BlockSpec((tm, d), lambda i: (i, 0)), _resident((1, d)), _resident((d, n))],
        out_specs=pl.BlockSpec((tm, n), lambda i: (i, 0)),
        out_shape=jax.ShapeDtypeStruct((r, n), out_dtype),
        compiler_params=_cparams(("parallel",)),
        name=name,
    )(h, g, w_bf)


def _mm_res_kernel(a_ref, w_ref, r_ref, o_ref):
    o_ref[...] = r_ref[...] + _dot(a_ref[...], w_ref[...])


def _mm_res(a_bf, w_bf, res, tm, name):
    r, k = a_bf.shape
    n = w_bf.shape[1]
    return pl.pallas_call(
        _mm_res_kernel,
        grid=(r // tm,),
        in_specs=[pl.BlockSpec((tm, k), lambda i: (i, 0)), _resident((k, n)),
                  pl.BlockSpec((tm, n), lambda i: (i, 0))],
        out_specs=pl.BlockSpec((tm, n), lambda i: (i, 0)),
        out_shape=jax.ShapeDtypeStruct((r, n), F32),
        compiler_params=_cparams(("parallel",)),
        name=name,
    )(a_bf, w_bf, res)


def _ffn_kernel(x_ref, g_ref, wg_ref, wu_ref, wd_ref, o_ref, *, f_chunk):
    x = x_ref[...]
    hn = _rms_rows(x, g_ref[...]).astype(BF16)
    acc = x
    for c in range(wg_ref.shape[1] // f_chunk):
        sl = slice(c * f_chunk, (c + 1) * f_chunk)
        gate = _dot(hn, wg_ref[:, sl])
        up = _dot(hn, wu_ref[:, sl])
        act = (gate / (1.0 + jnp.exp(-gate)) * up).astype(BF16)
        acc = acc + _dot(act, wd_ref[sl, :])
    o_ref[...] = acc


def _ffn(h, g, wg_bf, wu_bf, wd_bf, tm):
    r, d = h.shape
    f = wg_bf.shape[1]
    f_chunk = 256 if f % 256 == 0 else f
    return pl.pallas_call(
        functools.partial(_ffn_kernel, f_chunk=f_chunk),
        grid=(r // tm,),
        in_specs=[pl.BlockSpec((tm, d), lambda i: (i, 0)), _resident((1, d)),
                  _resident((d, f)), _resident((d, f)), _resident((f, d))],
        out_specs=pl.BlockSpec((tm, d), lambda i: (i, 0)),
        out_shape=jax.ShapeDtypeStruct((r, d), F32),
        compiler_params=_cparams(("parallel",)),
        name="swiglu",
    )(h, g, wg_bf, wu_bf, wd_bf)


def _rms_kernel(x_ref, g_ref, o_ref):
    o_ref[...] = _rms_rows(x_ref[...], g_ref[...])


def _rms(h, g, tm):
    r, d = h.shape
    return pl.pallas_call(
        _rms_kernel,
        grid=(r // tm,),
        in_specs=[pl.BlockSpec((tm, d), lambda i: (i, 0)), _resident((1, d))],
        out_specs=pl.BlockSpec((tm, d), lambda i: (i, 0)),
        out_shape=jax.ShapeDtypeStruct((r, d), F32),
        compiler_params=_cparams(("parallel",)),
        name="final_rms",
    )(h, g)


def _stack_slots(q):
    lane = lax.broadcasted_iota(jnp.int32, (1, LANES), 1)
    zero = jnp.zeros_like(q)
    return jnp.concatenate([jnp.where(lane < HEAD_DIM, q, zero),
                            jnp.where(lane >= HEAD_DIM, q, zero)], axis=0)


def _merge_slots(acc, tq):
    lane = lax.broadcasted_iota(jnp.int32, (1, LANES), 1)
    return jnp.where(lane < HEAD_DIM, acc[:tq], acc[tq:])


def _row_groups(tq, rg):
    return [(r0, slice(sl * tq + r0, sl * tq + r0 + rg)) for sl in range(2) for r0 in range(0, tq, rg)]


def _kv_block(ref, j, tk):
    return ref[pl.ds(pl.multiple_of(j * tk, tk), tk), :]


def _lane_tiles(x):
    return [x[:, t * LANES:(t + 1) * LANES] for t in range(x.shape[1] // LANES)]


def _with_ones(vb):
    return jnp.concatenate([vb, jnp.ones(vb.shape, vb.dtype)], axis=1)


def _softmax_update(s, vb1, rows, m_sc, acc_sc, first):
    tiles = _lane_tiles(s)
    m_cur = jnp.max(functools.reduce(jnp.maximum, tiles), axis=-1, keepdims=True)
    if first:
        m_new = jnp.broadcast_to(m_cur, tiles[0].shape)
    else:
        m_old = m_sc[rows, :]
        m_new = jnp.maximum(m_old, m_cur)
    p = jnp.concatenate([jnp.exp(t - m_new) for t in tiles], axis=1).astype(BF16)
    pv = _dot(p, vb1)
    if first:
        acc_sc[rows, :] = pv
    else:
        alpha = jnp.exp(m_old - m_new)
        acc_sc[rows, :] = jnp.concatenate([alpha, alpha], axis=1) * acc_sc[rows, :] + pv
    m_sc[rows, :] = m_new


def _diff_kernel(lam_ref, g_ref, q_ref, k_ref, v_ref, o_ref, m_sc, acc_sc, *, tq, tk, rg, out_scale):
    i = pl.program_id(2)
    qq = _stack_slots(q_ref[...])
    jd = (i * tq) // tk
    groups = _row_groups(tq, rg)

    def sweep(j, first):
        kb, vb1 = _kv_block(k_ref, j, tk), _with_ones(_kv_block(v_ref, j, tk))
        for r0, rows in groups:
            s = _dot_nt(qq[rows], kb)
            if first:
                qpos = i * tq + r0 + lax.broadcasted_iota(jnp.int32, (rg, tk), 0)
                kpos = jd * tk + lax.broadcasted_iota(jnp.int32, (rg, tk), 1)
                s = jnp.where(kpos <= qpos, s, NEG_INF)
            _softmax_update(s, vb1, rows, m_sc, acc_sc, first)

    sweep(jd, True)

    def body(j, carry):
        sweep(j, False)
        return carry

    lax.fori_loop(0, jd, body, 0)

    a = acc_sc[:, 0:LANES] / acc_sc[:, LANES:2 * LANES]
    o = a[:tq] - lam_ref[...] * a[tq:]
    o_ref[...] = (_rms_rows(o, g_ref[...]) * out_scale).astype(o_ref.dtype)


def _softplus(z):
    return jnp.maximum(z, 0.0) + jnp.log(1.0 + jnp.exp(-jnp.abs(z)))


def _stick_weights_tiled(z, tri, carry):
    sub = tri.shape[0]
    lstay = -_softplus(z)
    hi = lstay.astype(BF16)
    lo = (lstay - hi.astype(F32)).astype(BF16)
    out = [None] * (z.shape[1] // sub)
    for h in reversed(range(len(out))):
        cols = slice(h * sub, (h + 1) * sub)
        c_tiles = _lane_tiles(_dot(hi[:, cols], tri) + _dot(lo[:, cols], tri))
        if carry is not None:
            c_tiles = [c + carry for c in c_tiles]
        out[h] = jnp.exp(z[:, cols] + jnp.concatenate(c_tiles, axis=1))
        carry = jnp.broadcast_to(c_tiles[0][:, 0:1], c_tiles[0].shape)
    return jnp.concatenate(out, axis=1), carry


def _stick_kernel(tri_ref, q_ref, k_ref, v_ref, o_ref, c_sc, acc_sc, *, tq, tk, rg):
    i = pl.program_id(2)
    qq = _stack_slots(q_ref[...])
    tri = tri_ref[...]
    jd = (i * tq) // tk
    groups = _row_groups(tq, rg)

    def sweep(j, first):
        kb, vb = _kv_block(k_ref, j, tk), _kv_block(v_ref, j, tk)
        for r0, rows in groups:
            z = _dot_nt(qq[rows], kb)
            if first:
                qpos = i * tq + r0 + lax.broadcasted_iota(jnp.int32, (rg, tk), 0)
                kpos = jd * tk + lax.broadcasted_iota(jnp.int32, (rg, tk), 1)
                z = jnp.where(kpos < qpos, z, NEG_INF)
            w, c = _stick_weights_tiled(z, tri, None if first else c_sc[rows, :])
            c_sc[rows, :] = c
            pv = _dot(w.astype(BF16), vb)
            acc_sc[rows, :] = pv if first else acc_sc[rows, :] + pv

    sweep(jd, True)

    def live():
        return jnp.max(c_sc[...]) > STICK_DEAD

    def cond(state):
        t, alive = state
        return jnp.logical_and(t < jd, alive)

    def body(state):
        t, _ = state
        sweep(jd - 1 - t, False)
        return t + 1, live()

    lax.while_loop(cond, body, (jnp.int32(0), live()))
    o_ref[...] = _merge_slots(acc_sc[...], tq).astype(o_ref.dtype)


def _window_kernel(bias_ref, q_ref, k_ref, v_ref, o_ref, m_sc, acc_sc, *, tq, tk, rg):
    i = pl.program_id(2)
    qq = _stack_slots(q_ref[...])
    groups = _row_groups(tq, rg)
    ratio = tk // tq
    jd = (i * tq) // tk
    j_lo = jnp.maximum(i * tq - MAX_WINDOW, 0) // tk

    def sweep(j, first):
        kb, vb1 = _kv_block(k_ref, j, tk), _with_ones(_kv_block(v_ref, j, tk))
        for r0, rows in groups:
            s = _dot_nt(qq[rows], kb) + bias_ref[i - j * ratio, r0:r0 + rg, :]
            _softmax_update(s, vb1, rows, m_sc, acc_sc, first)

    sweep(jd, True)

    def body(t, carry):
        sweep(jd - 1 - t, False)
        return carry

    lax.fori_loop(0, jd - j_lo, body, 0)
    o_ref[...] = _merge_slots(acc_sc[:, 0:LANES] / acc_sc[:, LANES:2 * LANES], tq).astype(o_ref.dtype)


def _prompt_attn_specs(b, t, tq, col0):
    nq = t // tq
    qspec = pl.BlockSpec((tq, LANES), lambda bi, p, i: (bi * nq + i, col0 + p))
    kvspec = pl.BlockSpec((t, LANES), lambda bi, p, i: (bi, col0 + p))
    ospec = pl.BlockSpec((tq, LANES), lambda bi, p, i: (bi * nq + i, p))
    return nq, qspec, kvspec, ospec


def _diff_attn(q, k, v, lam_vec, g_vec, b, t, n_pairs, out_scale, tq, tk, rg):
    nq, qspec, kvspec, ospec = _prompt_attn_specs(b, t, tq, 0)
    return pl.pallas_call(
        functools.partial(_diff_kernel, tq=tq, tk=tk, rg=rg, out_scale=out_scale),
        grid=(b, n_pairs, nq),
        in_specs=[_resident((1, LANES)), _resident((1, LANES)), qspec, kvspec, kvspec],
        out_specs=ospec,
        out_shape=jax.ShapeDtypeStruct((b * t, n_pairs * LANES), BF16),
        scratch_shapes=[pltpu.VMEM((2 * tq, LANES), F32), pltpu.VMEM((2 * tq, 2 * LANES), F32)],
        compiler_params=_cparams(("parallel", "parallel", "arbitrary")),
        name=f"diff_attn_g{rg}",
    )(lam_vec, g_vec, q, k, v)


def _stick_attn(q, k, v, tri, b, t, n_pairs, col0, tq, tk, rg):
    nq, qspec, kvspec, ospec = _prompt_attn_specs(b, t, tq, col0)
    return pl.pallas_call(
        functools.partial(_stick_kernel, tq=tq, tk=tk, rg=rg),
        grid=(b, n_pairs, nq),
        in_specs=[_resident(tri.shape), qspec, kvspec, kvspec],
        out_specs=ospec,
        out_shape=jax.ShapeDtypeStruct((b * t, n_pairs * LANES), BF16),
        scratch_shapes=[pltpu.VMEM((2 * tq, LANES), F32), pltpu.VMEM((2 * tq, LANES), F32)],
        compiler_params=_cparams(("parallel", "parallel", "arbitrary")),
        name=f"stick_attn_g{rg}",
    )(tri, q, k, v)


def _window_attn(q, k, v, bias, b, t, n_pairs, tq, rg):
    nq, qspec, kvspec, ospec = _prompt_attn_specs(b, t, tq, 0)
    return pl.pallas_call(
        functools.partial(_window_kernel, tq=tq, tk=bias.shape[2], rg=rg),
        grid=(b, n_pairs, nq),
        in_specs=[_resident(bias.shape), qspec, kvspec, kvspec],
        out_specs=ospec,
        out_shape=jax.ShapeDtypeStruct((b * t, n_pairs * LANES), BF16),
        scratch_shapes=[pltpu.VMEM((2 * tq, LANES), F32), pltpu.VMEM((2 * tq, 2 * LANES), F32)],
        compiler_params=_cparams(("parallel", "parallel", "arbitrary")),
        name=f"window_attn_g{rg}",
    )(bias, q, k, v)


def _log_multiplicity(dist):
    cnt = jnp.zeros(dist.shape, F32)
    for window, dil in DILATED_PATTERNS:
        cnt = cnt + ((dist >= 0) & (dist <= window) & (dist % dil == 0)).astype(F32)
    return jnp.where(cnt > 0, jnp.log(jnp.maximum(cnt, 1.0)), NEG_INF)


def _slot_mask(n_rows, d, first_slot, width):
    r = lax.broadcasted_iota(jnp.int32, (n_rows, d), 0) + first_slot
    c = lax.broadcasted_iota(jnp.int32, (n_rows, d), 1)
    return jnp.right_shift(c, width.bit_length() - 1) == r


def _sample_kernel(*refs, even, ts, d, pps, out_scale):
    if even:
        _, lam_ref, g_ref, tri_ref = refs[:4]
        refs = refs[4:]
        n_soft = ts * 8
    else:
        n_soft = ts * 16
    q_ref, bn_ref, bc_ref, kn_ref, vn_ref = refs[:5]
    kc_refs, vc_refs = refs[5:5 + pps], refs[5 + pps:5 + 2 * pps]
    o_ref, qbd_sc, m_sc, l_sc, acc_sc = refs[5 + 2 * pps:10 + 2 * pps]
    c_sc = refs[10 + 2 * pps] if even else None
    c = pl.program_id(1)
    n_rows = ts * 16

    def process(kc, vc, bias, first):
        s = _dot_nt(qbd_sc[...], kc) + bias
        ss = s[:n_soft]
        if first:
            m_new = jnp.max(ss, axis=-1, keepdims=True)
            p = jnp.exp(ss - m_new)
            l_sc[...] = jnp.sum(p, axis=-1, keepdims=True)
        else:
            m_old = m_sc[...]
            m_new = jnp.maximum(m_old, jnp.max(ss, axis=-1, keepdims=True))
            alpha = jnp.exp(m_old - m_new)
            p = jnp.exp(ss - m_new)
            l_sc[...] = alpha * l_sc[...] + jnp.sum(p, axis=-1, keepdims=True)
        m_sc[...] = m_new
        if even:
            w, cnew = _stick_weights_tiled(s[n_soft:], tri_ref[...], None if first else c_sc[...])
            c_sc[...] = cnew
            p = jnp.concatenate([p, w], axis=0)
        pv = _dot(p.astype(BF16), vc)
        if first:
            acc_sc[...] = pv
        else:
            acc_sc[0:n_soft] = alpha * acc_sc[0:n_soft] + pv[:n_soft]
            if even:
                acc_sc[n_soft:n_rows] = acc_sc[n_soft:n_rows] + pv[n_soft:]

    @pl.when(c == 0)
    def _():
        q = q_ref[0]
        if even:
            md = _slot_mask(8, d, 0, HEAD_DIM)
            mb = _slot_mask(8, d, 8, HEAD_DIM)
            rows = [jnp.where(md, jnp.broadcast_to(q[t:t + 1], (8, d)), 0.0) for t in range(ts)]
            rows += [jnp.where(mb, jnp.broadcast_to(q[t:t + 1], (8, d)), 0.0) for t in range(ts)]
        else:
            ma = _slot_mask(16, d, 0, HEAD_DIM)
            rows = [jnp.where(ma, jnp.broadcast_to(q[t:t + 1], (16, d)), 0.0) for t in range(ts)]
        qbd_sc[...] = jnp.concatenate(rows, axis=0).astype(BF16)
        process(kn_ref[0], vn_ref[0], bn_ref[...], True)

    @pl.when(c > 0)
    def _():
        kc = jnp.concatenate([r[...].astype(BF16) for r in kc_refs], axis=0)
        vc = jnp.concatenate([r[...].astype(BF16) for r in vc_refs], axis=0)
        process(kc, vc, bc_ref[0], False)

    @pl.when(c == pl.num_programs(1) - 1)
    def _():
        acc = acc_sc[...]
        if even:
            a = (acc[:n_soft] / l_sc[...]).reshape(ts, 8, d)
            r = lax.broadcasted_iota(jnp.int32, (8, d), 0)
            col = lax.broadcasted_iota(jnp.int32, (8, d), 1)
            head2 = 2 * jnp.right_shift(col, (2 * HEAD_DIM).bit_length() - 1)
            pick0 = (r == head2)[None]
            pick1 = (r == head2 + 1)[None]
            o0 = jnp.sum(jnp.where(pick0, a, 0.0), axis=1)
            o1 = jnp.sum(jnp.where(pick1, a, 0.0), axis=1)
            ob = jnp.sum(jnp.where(_slot_mask(8, d, 8, HEAD_DIM)[None],
                                   acc[n_soft:].reshape(ts, 8, d), 0.0), axis=1)
            half = d // 2
            for hd in range(half // LANES):
                sl = slice(hd * LANES, (hd + 1) * LANES)
                oa = o0[:, sl] - lam_ref[...] * o1[:, sl]
                o_ref[0, :, sl] = (_rms_rows(oa, g_ref[...]) * out_scale).astype(o_ref.dtype)
            o_ref[0, :, half:] = ob[:, half:].astype(o_ref.dtype)
        else:
            a = (acc / l_sc[...]).reshape(ts, 16, d)
            o = jnp.sum(jnp.where(_slot_mask(16, d, 0, HEAD_DIM)[None], a, 0.0), axis=1)
            o_ref[0] = o.astype(o_ref.dtype)


def _sample_attn(q3, bias_new, bias_cache, k_new, v_new, cache_k, cache_v, layer, *, even, page_table=None,
                 lam_vec=None, g_vec=None, tri=None, out_scale=1.0):
    s, ts, d = q3.shape
    n_groups, n_rows, gk = bias_cache.shape
    pps = gk // LANES
    group = lambda c: n_groups - jnp.maximum(c, 1)
    if even:
        maps = lambda f: (lambda si, c, pt: f(si, c, pt))
        cache_maps = [lambda si, c, pt, p=p: (layer, pt[si, group(c) * pps + p], 0, 0) for p in range(pps)]
    else:
        maps = lambda f: (lambda si, c: f(si, c, None))
        cache_maps = [lambda si, c, p=p: (layer, si, group(c) * pps + p, 0) for p in range(pps)]
    seq3 = maps(lambda si, c, pt: (si, 0, 0))
    const2 = maps(lambda si, c, pt: (0, 0))
    grp3 = maps(lambda si, c, pt: (jnp.maximum(c, 1) - 1, 0, 0))
    cache_specs = [pl.BlockSpec((None, None, LANES, d), m) for m in cache_maps]
    in_specs = [pl.BlockSpec((1, ts, d), seq3), pl.BlockSpec((n_rows, LANES), const2),
                pl.BlockSpec((1, n_rows, gk), grp3),
                pl.BlockSpec((1, LANES, d), seq3), pl.BlockSpec((1, LANES, d), seq3)]
    in_specs += cache_specs + cache_specs
    args = [q3, bias_new, bias_cache, k_new, v_new] + [cache_k] * pps + [cache_v] * pps
    scratch = [pltpu.VMEM((n_rows, d), BF16),
               pltpu.VMEM((ts * 8 if even else n_rows, 1), F32),
               pltpu.VMEM((ts * 8 if even else n_rows, 1), F32),
               pltpu.VMEM((n_rows, d), F32)]
    if even:
        in_specs = [pl.BlockSpec((1, LANES), const2), pl.BlockSpec((1, LANES), const2),
                    pl.BlockSpec((LANES, LANES), const2)] + in_specs
        args = [page_table, lam_vec, g_vec, tri] + args
        scratch.append(pltpu.VMEM((ts * 8, LANES), F32))
    grid_spec = pltpu.PrefetchScalarGridSpec(
        num_scalar_prefetch=1 if even else 0,
        grid=(s, n_groups + 1),
        in_specs=in_specs,
        out_specs=pl.BlockSpec((1, ts, d), seq3),
        scratch_shapes=scratch,
    )
    return pl.pallas_call(
        functools.partial(_sample_kernel, even=even, ts=ts, d=d, pps=pps, out_scale=out_scale),
        grid_spec=grid_spec,
        out_shape=jax.ShapeDtypeStruct((s, ts, d), F32),
        compiler_params=_cparams(("parallel", "arbitrary")),
        name="sample_attn_even" if even else "sample_attn_odd",
    )(*args)


def _mem_heads(q, k, v, o_store):
    d = q.shape[-1]
    hd = d // MEM_HEADS
    for h in range(MEM_HEADS):
        sl = slice(h * hd, (h + 1) * hd)
        s = _dot_nt(q[:, sl].astype(BF16), k[:, sl]) * (hd ** -0.5)
        p = jnp.exp(s - jnp.max(s, axis=-1, keepdims=True))
        o = _dot(p.astype(BF16), v[:, sl]) / jnp.sum(p, axis=-1, keepdims=True)
        o_store(sl, o)


def _mem_prompt_kernel(q_ref, k_ref, v_ref, o_ref):
    def store(sl, o):
        o_ref[:, sl] = o.astype(o_ref.dtype)
    _mem_heads(q_ref[...], k_ref[...], v_ref[...], store)


def _mem_prompt(q, mk_bf, mv_bf, b, t, tq):
    d = q.shape[1]
    n_mem = mk_bf.shape[0] // b
    nq = t // tq
    return pl.pallas_call(
        _mem_prompt_kernel,
        grid=(b, nq),
        in_specs=[pl.BlockSpec((tq, d), lambda bi, i: (bi * nq + i, 0)),
                  pl.BlockSpec((n_mem, d), lambda bi, i: (bi, 0)),
                  pl.BlockSpec((n_mem, d), lambda bi, i: (bi, 0))],
        out_specs=pl.BlockSpec((tq, d), lambda bi, i: (bi * nq + i, 0)),
        out_shape=jax.ShapeDtypeStruct((b * t, d), BF16),
        compiler_params=_cparams(("parallel", "parallel")),
        name="mem_attn_prompt",
    )(q, mk_bf, mv_bf)


def _mem_sample_kernel(q_ref, k_ref, v_ref, o_ref, *, n_seq):
    for si in range(n_seq):
        def store(sl, o, si=si):
            o_ref[si, :, sl] = o.astype(o_ref.dtype)
        _mem_heads(q_ref[si], k_ref[si].astype(BF16), v_ref[si].astype(BF16), store)


def _mem_sample(q3, cache_k, cache_v, layer, n_seq):
    s, ts, d = q3.shape
    n_mem = cache_k.shape[2]
    kv = pl.BlockSpec((None, n_seq, n_mem, d), lambda i: (layer, i, 0, 0))
    return pl.pallas_call(
        functools.partial(_mem_sample_kernel, n_seq=n_seq),
        grid=(s // n_seq,),
        in_specs=[pl.BlockSpec((n_seq, ts, d), lambda i: (i, 0, 0)), kv, kv],
        out_specs=pl.BlockSpec((n_seq, ts, d), lambda i: (i, 0, 0)),
        out_shape=jax.ShapeDtypeStruct((s, ts, d), F32),
        compiler_params=_cparams(("parallel",)),
        name="mem_attn_sample",
    )(q3, cache_k, cache_v)


def _rope_tables(pos):
    half = ROT_DIM // 2
    inv = jnp.power(jnp.float32(ROPE_THETA), -jnp.arange(half, dtype=F32) / half)
    ang = pos.astype(F32)[:, None] * inv[None, :]
    cos, sin = jnp.cos(ang), jnp.sin(ang)
    n = pos.shape[0]
    pad = jnp.zeros((n, HEAD_DIM - ROT_DIM), F32)
    z = jnp.zeros((n, half), F32)
    c_head = jnp.concatenate([cos, cos, pad + 1.0], axis=1)
    lo_head = jnp.concatenate([z, sin, pad], axis=1)
    hi_head = jnp.concatenate([-sin, z, pad], axis=1)
    rep = LANES // HEAD_DIM
    return tuple(jnp.tile(a, (1, rep)) for a in (c_head, lo_head, hi_head))


def kernel(x_prompt, x_sample, cache_k_even, cache_v_even, cache_k_odd, cache_v_odd, cache_mem_k, cache_mem_v, page_table, mem_prompt, w_in, w_out, diff_lambda, diff_subln, norm_mix, norm_mem_q, norm_mem_kv, w_q_mem, w_kv_mem, w_o_mem, norm_ffn, w_gate_up, w_down, norm_final):
    b, t, d = x_prompt.shape
    s, ts, _ = x_sample.shape
    depth = w_in.shape[0]
    n_heads = d // HEAD_DIM
    n_pages, page = page_table.shape[1], cache_k_even.shape[2]
    past = n_pages * page
    cwin = cache_k_odd.shape[2]
    n_mem = mem_prompt.shape[1]
    d_ff = w_down.shape[1]
    assert d % (2 * LANES) == 0 and page == LANES and cwin % LANES == 0 and past == cwin

    rp, rs = b * t, s * ts
    tm = 512
    r_pad = -(-(rp + rs) // tm) * tm
    tq = min(256, t)
    tk = min(512, t)
    assert t % tq == 0 and t % tk == 0 and tk % tq == 0 and rp % tm == 0

    h = jnp.concatenate([x_prompt.reshape(rp, d), x_sample.reshape(rs, d),
                         jnp.zeros((r_pad - rp - rs, d), F32)], axis=0)
    pos = jnp.concatenate([jnp.tile(jnp.arange(t, dtype=jnp.int32), b),
                           jnp.tile(past + jnp.arange(ts, dtype=jnp.int32), s),
                           jnp.zeros((r_pad - rp - rs,), jnp.int32)])
    rope_tabs = _rope_tables(pos)

    sub = min(2 * LANES, tk)
    tri_p = jnp.tril(jnp.ones((sub, sub), F32)).astype(BF16)
    tri_s = jnp.tril(jnp.ones((LANES, LANES), F32)).astype(BF16)
    n_tab = (MAX_WINDOW + tk - 1) // tq + 1
    dist = (jnp.arange(n_tab, dtype=jnp.int32)[:, None, None] * tq
            + jnp.arange(tq, dtype=jnp.int32)[None, :, None] - jnp.arange(tk, dtype=jnp.int32)[None, None, :])
    bias_window = _log_multiplicity(dist)

    n_chunks = past // LANES
    qi = jnp.arange(ts, dtype=jnp.int32)
    key = jnp.arange(LANES, dtype=jnp.int32)
    new_valid = key[None, :] < ts
    causal_new = jnp.where(new_valid & (key[None, :] <= qi[:, None]), 0.0, NEG_INF)
    strict_new = jnp.where(new_valid & (key[None, :] < qi[:, None]), 0.0, NEG_INF)
    rows8 = lambda a: jnp.repeat(a, 8, axis=0)
    pps_even = max(1, min(8, n_chunks // 2))
    assert n_chunks % pps_even == 0
    bias_even_new = jnp.concatenate([rows8(causal_new), rows8(strict_new)], axis=0).astype(F32)
    bias_even_cache = jnp.zeros((n_chunks // pps_even, ts * 16, pps_even * LANES), F32)
    new_dist = jnp.where(new_valid, qi[:, None] - key[None, :], -1)
    big_dil = DILATED_PATTERNS[-1][1]
    near_len = cwin
    far_len = cwin - near_len
    assert cwin % big_dil == 0 and far_len % big_dil == 0 and ts <= big_dil
    far_pos = (jnp.arange(far_len // big_dil, dtype=jnp.int32)[:, None] * big_dil
               + jnp.arange(ts, dtype=jnp.int32)[None, :]).reshape(-1)
    sel_pos = (past - cwin) + jnp.concatenate([far_pos, far_len + jnp.arange(near_len, dtype=jnp.int32)])
    n_sel = sel_pos.shape[0]
    assert n_sel % LANES == 0
    sel_dist = past + qi[:, None] - sel_pos[None, :]
    sel_chunks = n_sel // LANES
    pps_odd = max(p for p in range(1, 9) if sel_chunks % p == 0)
    bias_odd_cache = _log_multiplicity(sel_dist).reshape(ts, sel_chunks // pps_odd, pps_odd * LANES)
    bias_odd_cache = jnp.repeat(bias_odd_cache.transpose(1, 0, 2)[::-1], 16, axis=1)
    bias_odd_new = jnp.repeat(_log_multiplicity(new_dist), 16, axis=0)

    def select_rows(cache):
        n = cache.shape[0]
        if far_len == 0:
            return cache.reshape(n, s, cwin, d)
        far = cache[:, :, :far_len].reshape(n, s, far_len // big_dil, big_dil, d)[:, :, :, :ts]
        return jnp.concatenate([far.reshape(n, s, -1, d), cache[:, :, far_len:].reshape(n, s, near_len, d)], axis=2)

    cke = cache_k_even.reshape(cache_k_even.shape[:3] + (d,))
    cve = cache_v_even.reshape(cache_v_even.shape[:3] + (d,))
    cko = select_rows(cache_k_odd)
    cvo = select_rows(cache_v_odd)
    cmk = cache_mem_k.reshape(cache_mem_k.shape[:3] + (d,))
    cmv = cache_mem_v.reshape(cache_mem_v.shape[:3] + (d,))
    mem_rows = mem_prompt.reshape(b * n_mem, d)
    mem_tm = min(tm, b * n_mem)

    half = d // 2
    even_cols = jnp.concatenate([jnp.arange(half) + o * half for o in (0, 3, 1, 4, 2, 5)])

    def pad_new(x):
        x3 = x[rp:rp + rs].reshape(s, ts, d)
        return jnp.concatenate([x3, jnp.zeros((s, LANES - ts, d), x.dtype)], axis=1)

    outs = {n: [] for n in ("kep", "vep", "kes", "ves", "kop", "vop", "kos", "vos", "mkp", "mvp")}
    keep = min(MAX_WINDOW, t)

    for l in range(depth):
        even = l % 2 == 0
        w_l = w_in[l][:, even_cols] if even else w_in[l]
        q, kf, vf, kb, vb = _proj(h, norm_mix[l][None], w_l.astype(BF16), rope_tabs,
                                  half if even else d, tm)
        k_heads = kf[:rp].reshape(b, t, n_heads, HEAD_DIM)
        v_heads = vf[:rp].reshape(b, t, n_heads, HEAD_DIM)
        ks_heads = kf[rp:rp + rs].reshape(s, ts, n_heads, HEAD_DIM)
        vs_heads = vf[rp:rp + rs].reshape(s, ts, n_heads, HEAD_DIM)
        q_s = q[rp:rp + rs].reshape(s, ts, d).astype(F32)
        if even:
            e = l // 2
            lam_init = 0.8 - 0.6 * math.exp(-0.3 * l)
            lp = diff_lambda[e].astype(F32)
            lam = jnp.exp(jnp.sum(lp[0] * lp[1])) - jnp.exp(jnp.sum(lp[2] * lp[3])) + lam_init
            lam_vec = jnp.full((1, LANES), lam, F32)
            g_vec = diff_subln[e][None].astype(F32)
            n_pairs = half // LANES
            oa = _diff_attn(q, kb, vb, lam_vec, g_vec, b, t, n_pairs, 1.0 - lam_init, tq, tk, min(tq, 128))
            ob = _stick_attn(q, kb, vb, tri_p, b, t, n_pairs, n_pairs, tq, tk, tq)
            attn_p = jnp.concatenate([oa, ob], axis=1)
            attn_s = _sample_attn(q_s, bias_even_new, bias_even_cache, pad_new(kb), pad_new(vb), cke, cve, e, even=True,
                                  page_table=page_table, lam_vec=lam_vec, g_vec=g_vec, tri=tri_s,
                                  out_scale=1.0 - lam_init)
            outs["kep"].append(k_heads); outs["vep"].append(v_heads)
            outs["kes"].append(ks_heads); outs["ves"].append(vs_heads)
        else:
            o_idx = l // 2
            attn_p = _window_attn(q, kb, vb, bias_window, b, t, d // LANES, tq, min(tq, 128))
            attn_s = _sample_attn(q_s, bias_odd_new, bias_odd_cache, pad_new(kb), pad_new(vb), cko, cvo, o_idx,
                                  even=False)
            outs["kop"].append(k_heads[:, t - keep:]); outs["vop"].append(v_heads[:, t - keep:])
            outs["kos"].append(ks_heads); outs["vos"].append(vs_heads)
        attn = jnp.concatenate([attn_p, attn_s.reshape(rs, d).astype(BF16),
                                jnp.zeros((r_pad - rp - rs, d), BF16)], axis=0)
        h = _mm_res(attn, w_out[l].astype(BF16), h, tm, "attn_out_proj")

        mkv = _rms_mm(mem_rows, norm_mem_kv[l][None], w_kv_mem[l].astype(BF16), F32, mem_tm, "mem_kv_proj")
        mk, mv = mkv[:, :d], mkv[:, d:]
        outs["mkp"].append(mk.reshape(b, n_mem, MEM_HEADS, d // MEM_HEADS))
        outs["mvp"].append(mv.reshape(b, n_mem, MEM_HEADS, d // MEM_HEADS))
        qm = _rms_mm(h, norm_mem_q[l][None], w_q_mem[l].astype(BF16), BF16, tm, "mem_q_proj")
        om_p = _mem_prompt(qm, mk.astype(BF16), mv.astype(BF16), b, t, min(512, t))
        om_s = _mem_sample(qm[rp:rp + rs].reshape(s, ts, d).astype(F32), cmk, cmv, l, 2 if s % 2 == 0 else 1)
        om = jnp.concatenate([om_p, om_s.reshape(rs, d).astype(BF16),
                              jnp.zeros((r_pad - rp - rs, d), BF16)], axis=0)
        h = _mm_res(om, w_o_mem[l].astype(BF16), h, tm, "mem_out_proj")

        wgu = w_gate_up[l].astype(BF16)
        h = _ffn(h, norm_ffn[l][None], wgu[:, :d_ff], wgu[:, d_ff:], w_down[l].astype(BF16), tm)

    y = _rms(h, norm_final[None], tm)
    st = lambda n: jnp.stack(outs[n])
    return (y[:rp].reshape(b, t, d), y[rp:rp + rs].reshape(s, ts, d),
            st("kep"), st("vep"), st("kes"), st("ves"),
            st("kop"), st("vop"), st("kos"), st("vos"), st("mkp"), st("mvp"))
```
